```python
import math
import jax, jax.numpy as jnp
from jax import lax
import numpy as np

D_MODEL = 4096
BATCH = 4
SEQ = 2048
DEPTH = 2
DEC_BATCH = 32
DEC_SEQ = 4
PAST_LEN = 16384
PAGE_SIZE = 128

DN_HEADS = 16
DN_DK = 128
DN_DV = 128
DN_CONV = 4
DN_CHUNK = 64
DN_QK_W = DN_HEADS * DN_DK
DN_V_W = DN_HEADS * DN_DV
DN_CONV_DIM = 2 * DN_QK_W + DN_V_W
SW_HEADS = 32
SW_KV_HEADS = 8
SW_GROUP = SW_HEADS // SW_KV_HEADS
SW_HD = 64
SW_Q_W = SW_HEADS * SW_HD
SW_KV_W = SW_KV_HEADS * SW_HD
WINDOW = 128
D_FF = 11008
EPS = 1e-6
IN_SPLITS = (DN_CONV_DIM, DN_V_W, DN_HEADS, DN_HEADS, SW_Q_W, SW_KV_W, SW_KV_W, D_MODEL, D_MODEL)
IN_W = sum(IN_SPLITS)

kernel_name = 'hybrid_gdn_swa_macaron_step'


def rmsnorm(x, w):
    xf = x.astype(jnp.float32)
    y = xf * lax.rsqrt(jnp.mean(xf * xf, -1, keepdims=True) + EPS)
    return (y * w.astype(jnp.float32)).astype(x.dtype)


def swiglu(x, wg, wu, wd):
    return (jax.nn.silu(x @ wg) * (x @ wu)) @ wd


def l2norm(x):
    return x * lax.rsqrt(jnp.sum(x * x, -1, keepdims=True) + 1e-6)


def alibi_slopes(n):
    return 2.0 ** (-8.0 * jnp.arange(1, n + 1, dtype=jnp.float32) / n)


def split_in(p):
    idx, acc = [], 0
    for s in IN_SPLITS[:-1]:
        acc += s
        idx.append(acc)
    return jnp.split(p, idx, axis=-1)


def short_conv(u, buf, w):
    T = u.shape[1]
    ext = jnp.concatenate([buf.astype(u.dtype), u], axis=1)
    y = sum(ext[:, j:j + T] * w[j] for j in range(DN_CONV))
    return jax.nn.silu(y), ext[:, -(DN_CONV - 1):]


def gated_delta_chunked(q, k, v, g, beta, s0):
    B, T, H, dk = q.shape
    dv = v.shape[-1]
    C = min(DN_CHUNK, T)
    N = -(-T // C)
    pad = N * C - T
    def to_chunks(a):
        a = jnp.pad(a, [(0, 0), (0, pad)] + [(0, 0)] * (a.ndim - 2))
        return jnp.moveaxis(a.reshape((B, N, C) + a.shape[2:]), 3, 1)
    qc, kc, vc, gc, bc = (to_chunks(a) for a in (q, k, v, g, beta))
    G = jnp.cumsum(gc, axis=-1)
    causal = jnp.tril(jnp.ones((C, C), bool))
    strict = jnp.tril(jnp.ones((C, C), bool), -1)
    decay = jnp.exp(jnp.where(causal, G[..., :, None] - G[..., None, :], -jnp.inf))
    kb = kc * bc[..., None]
    A = jnp.where(strict, jnp.einsum('bhncd,bhnsd->bhncs', kb, kc) * decay, 0.0)
    eye = jnp.eye(C, dtype=A.dtype)
    rhs = jnp.concatenate([vc * bc[..., None], kb * jnp.exp(G)[..., None]], axis=-1)
    sol = lax.linalg.triangular_solve(eye + A, rhs, left_side=True, lower=True, unit_diagonal=True)
    u_base, w_cum = sol[..., :dv], sol[..., dv:]
    a_qk = jnp.einsum('bhncd,bhnsd->bhncs', qc, kc) * decay
    q_dec = qc * jnp.exp(G)[..., None]
    k_dec = kc * jnp.exp(G[..., -1:] - G)[..., None]
    g_tot = jnp.exp(G[..., -1])

    def step(S, xs):
        u_b, w_c, aq, q_d, k_d, gt = xs
        u = u_b - jnp.einsum('bhck,bhkv->bhcv', w_c, S)
        o = jnp.einsum('bhck,bhkv->bhcv', q_d, S) + jnp.einsum('bhcs,bhsv->bhcv', aq, u)
        S = S * gt[..., None, None] + jnp.einsum('bhck,bhcv->bhkv', k_d, u)
        return S, o

    xs = tuple(jnp.moveaxis(a, 2, 0) for a in (u_base, w_cum, a_qk, q_dec, k_dec, g_tot))
    S, o = lax.scan(step, s0, xs)
    o = jnp.transpose(o, (1, 0, 3, 2, 4)).reshape(B, N * C, H, dv)[:, :T]
    return o, S


def deltanet_branch(qkv_raw, z, b_raw, a_raw, conv_buf, s0, conv_w, a_log, dt_bias, norm_w):
    B, T, _ = qkv_raw.shape
    f32 = jnp.float32
    qkv, new_buf = short_conv(qkv_raw, conv_buf, conv_w)
    q, k, v = jnp.split(qkv.astype(f32), [DN_QK_W, 2 * DN_QK_W], axis=-1)
    q = l2norm(q.reshape(B, T, DN_HEADS, DN_DK)) * (DN_DK ** -0.5)
    k = l2norm(k.reshape(B, T, DN_HEADS, DN_DK))
    v = v.reshape(B, T, DN_HEADS, DN_DV)
    beta = jax.nn.sigmoid(b_raw.astype(f32))
    g = -jnp.exp(a_log.astype(f32)) * jax.nn.softplus(a_raw.astype(f32) + dt_bias.astype(f32))
    o, s_new = gated_delta_chunked(q, k, v, g, beta, s0.astype(f32))
    o = o * lax.rsqrt(jnp.mean(o * o, -1, keepdims=True) + EPS) * norm_w.astype(f32)
    o = o * jax.nn.silu(z.astype(f32).reshape(B, T, DN_HEADS, DN_DV))
    return o.reshape(B, T, DN_V_W).astype(qkv_raw.dtype), new_buf, s_new


def swa_attend(q, k, v, q_pos, k_pos, sinks, slopes):
    Bt, Tq = q.shape[:2]
    f32 = jnp.float32
    qg = q.reshape(Bt, Tq, SW_KV_HEADS, SW_GROUP, SW_HD)
    s = jnp.einsum('bqkgd,bskd->bkgqs', qg, k).astype(f32) * (SW_HD ** -0.5)
    dist = q_pos[:, :, None] - k_pos[:, None, :]
    valid = (dist >= 0) & (dist <= WINDOW) & (k_pos[:, None, :] >= 0)
    bias = -slopes.reshape(SW_KV_HEADS, SW_GROUP)[None, :, :, None, None] * dist[:, None, None].astype(f32)
    s = jnp.where(valid[:, None, None], s + bias, -jnp.inf)
    sink = sinks.astype(f32).reshape(SW_KV_HEADS, SW_GROUP)[None, :, :, None, None]
    m = jnp.maximum(jnp.max(s, -1, keepdims=True), sink)
    p = jnp.exp(s - m)
    p = p / (jnp.sum(p, -1, keepdims=True) + jnp.exp(sink - m))
    o = jnp.einsum('bkgqs,bskd->bqkgd', p.astype(v.dtype), v)
    return o.reshape(Bt, Tq, SW_Q_W)


def swa_prompt(q, k, v, sinks, slopes):
    B, T = q.shape[:2]
    W = WINDOW
    N = -(-T // W)
    Tp = N * W
    qp = jnp.pad(q, ((0, 0), (0, Tp - T), (0, 0), (0, 0))).reshape(B * N, W, SW_HEADS, SW_HD)
    def band(a):
        a = jnp.pad(a, ((0, 0), (W, Tp - T), (0, 0), (0, 0))).reshape(B, N + 1, W, SW_KV_HEADS, SW_HD)
        return jnp.concatenate([a[:, :-1], a[:, 1:]], axis=2).reshape(B * N, 2 * W, SW_KV_HEADS, SW_HD)
    start = jnp.arange(N, dtype=jnp.int32)[:, None] * W
    q_pos = jnp.tile(start + jnp.arange(W, dtype=jnp.int32), (B, 1))
    k_pos = jnp.tile(start - W + jnp.arange(2 * W, dtype=jnp.int32), (B, 1))
    o = swa_attend(qp, band(k), band(v), q_pos, k_pos, sinks, slopes)
    return o.reshape(B, Tp, SW_Q_W)[:, :T]


def swa_decode(q, k, v, k_buf, v_buf, sinks, slopes):
    B, T = q.shape[:2]
    L = k_buf.shape[1]
    k_all = jnp.concatenate([k_buf.astype(k.dtype), k], axis=1)
    v_all = jnp.concatenate([v_buf.astype(v.dtype), v], axis=1)
    q_pos = (L + jnp.arange(T, dtype=jnp.int32))[None]
    k_pos = jnp.arange(L + T, dtype=jnp.int32)[None]
    o = swa_attend(q, k_all, v_all, q_pos, k_pos, sinks, slopes)
    return o, k_all[:, -L:], v_all[:, -L:]


def trunk_layer(x, conv_buf, dn_state, k_buf, v_buf, slopes,
                ln_f1, f1_g, f1_u, f1_d, ln_mix, w_in, conv_w, a_log, dt_bias, dn_norm,
                sinks, w_br_a, w_br_b, w_o, ln_f2, f2_g, f2_u, f2_d):
    B, T, _ = x.shape
    x = x + 0.5 * swiglu(rmsnorm(x, ln_f1), f1_g, f1_u, f1_d)
    h = rmsnorm(x, ln_mix)
    qkv_raw, z, b_raw, a_raw, sq, sk, sv, ga, gb = split_in(h @ w_in)
    if conv_buf is None:
        conv_buf = jnp.zeros((B, DN_CONV - 1, DN_CONV_DIM), x.dtype)
    if dn_state is None:
        dn_state = jnp.zeros((B, DN_HEADS, DN_DK, DN_DV), jnp.float32)
    o_a, new_conv, new_s = deltanet_branch(qkv_raw, z, b_raw, a_raw, conv_buf, dn_state,
                                           conv_w, a_log, dt_bias, dn_norm)
    sq = sq.reshape(B, T, SW_HEADS, SW_HD)
    sk = sk.reshape(B, T, SW_KV_HEADS, SW_HD)
    sv = sv.reshape(B, T, SW_KV_HEADS, SW_HD)
    if k_buf is None:
        o_b = swa_prompt(sq, sk, sv, sinks, slopes)
        keep = min(WINDOW, T)
        new_k, new_v = sk[:, -keep:], sv[:, -keep:]
    else:
        o_b, new_k, new_v = swa_decode(sq, sk, sv, k_buf, v_buf, sinks, slopes)
    merged = jax.nn.sigmoid(ga) * (o_a @ w_br_a) + jax.nn.sigmoid(gb) * (o_b @ w_br_b)
    x = x + merged @ w_o
    x = x + 0.5 * swiglu(rmsnorm(x, ln_f2), f2_g, f2_u, f2_d)
    return x, new_conv, new_s, new_k, new_v


def setup_inputs(seed: int = 0) -> dict:
    key = jax.random.key(seed)
    ks = iter(jax.random.split(key, 40))
    f32 = jnp.float32
    def nrm(shape, scale=1.0):
        return jax.random.normal(next(ks), shape, f32) * scale
    def gain(shape):
        return 1.0 + 0.01 * jax.random.normal(next(ks), shape, f32)
    buf_len = min(WINDOW, PAST_LEN)
    dt = jnp.exp(jax.random.uniform(next(ks), (DEPTH, DN_HEADS), f32, math.log(1e-3), math.log(1e-1)))
    a_log = jnp.log(jax.random.uniform(next(ks), (DEPTH, DN_HEADS), f32, 1.0, 16.0))
    return {
        'x_prompt': nrm((BATCH, SEQ, D_MODEL)),
        'x_sample': nrm((DEC_BATCH, DEC_SEQ, D_MODEL)),
        'state_dn_conv': nrm((DEPTH, DEC_BATCH, DN_CONV - 1, DN_CONV_DIM)),
        'state_dn_recurrent': nrm((DEPTH, DEC_BATCH, DN_HEADS, DN_DK, DN_DV), DN_DK ** -0.5),
        'cache_swa_k': nrm((DEPTH, DEC_BATCH, buf_len, SW_KV_HEADS, SW_HD)),
        'cache_swa_v': nrm((DEPTH, DEC_BATCH, buf_len, SW_KV_HEADS, SW_HD)),
        'ln_ffn1': gain((DEPTH, D_MODEL)),
        'w_ffn1_gate': nrm((DEPTH, D_MODEL, D_FF), D_MODEL ** -0.5),
        'w_ffn1_up': nrm((DEPTH, D_MODEL, D_FF), D_MODEL ** -0.5),
        'w_ffn1_down': nrm((DEPTH, D_FF, D_MODEL), D_FF ** -0.5),
        'ln_mix': gain((DEPTH, D_MODEL)),
        'w_in': nrm((DEPTH, D_MODEL, IN_W), D_MODEL ** -0.5),
        'dn_conv_w': nrm((DEPTH, DN_CONV, DN_CONV_DIM), DN_CONV ** -0.5),
        'dn_a_log': a_log,
        'dn_dt_bias': dt + jnp.log(-jnp.expm1(-dt)),
        'dn_norm_w': gain((DEPTH, DN_DV)),
        'swa_sinks': nrm((DEPTH, SW_HEADS)),
        'w_branch_a': nrm((DEPTH, DN_V_W, D_MODEL), DN_V_W ** -0.5),
        'w_branch_b': nrm((DEPTH, SW_Q_W, D_MODEL), SW_Q_W ** -0.5),
        'w_out': nrm((DEPTH, D_MODEL, D_MODEL), D_MODEL ** -0.5),
        'ln_ffn2': gain((DEPTH, D_MODEL)),
        'w_ffn2_gate': nrm((DEPTH, D_MODEL, D_FF), D_MODEL ** -0.5),
        'w_ffn2_up': nrm((DEPTH, D_MODEL, D_FF), D_MODEL ** -0.5),
        'w_ffn2_down': nrm((DEPTH, D_FF, D_MODEL), D_FF ** -0.5),
        'ln_final': gain((D_MODEL,)),
    }


def reference(x_prompt, x_sample, state_dn_conv, state_dn_recurrent, cache_swa_k, cache_swa_v,
              ln_ffn1, w_ffn1_gate, w_ffn1_up, w_ffn1_down, ln_mix, w_in, dn_conv_w, dn_a_log,
              dn_dt_bias, dn_norm_w, swa_sinks, w_branch_a, w_branch_b, w_out, ln_ffn2,
              w_ffn2_gate, w_ffn2_up, w_ffn2_down, ln_final):
    slopes = alibi_slopes(SW_HEADS)
    xp, xs = x_prompt, x_sample
    p_conv, p_rec, p_k, p_v = [], [], [], []
    s_conv, s_rec, s_k, s_v = [], [], [], []
    for l in range(DEPTH):
        lw = (ln_ffn1[l], w_ffn1_gate[l], w_ffn1_up[l], w_ffn1_down[l], ln_mix[l], w_in[l],
              dn_conv_w[l], dn_a_log[l], dn_dt_bias[l], dn_norm_w[l], swa_sinks[l],
              w_branch_a[l], w_branch_b[l], w_out[l], ln_ffn2[l], w_ffn2_gate[l], w_ffn2_up[l],
              w_ffn2_down[l])
        xp, c, r, kk, vv = trunk_layer(xp, None, None, None, None, slopes, *lw)
        p_conv.append(c); p_rec.append(r); p_k.append(kk); p_v.append(vv)
        xs, c, r, kk, vv = trunk_layer(xs, state_dn_conv[l], state_dn_recurrent[l],
                                       cache_swa_k[l], cache_swa_v[l], slopes, *lw)
        s_conv.append(c); s_rec.append(r); s_k.append(kk); s_v.append(vv)
    y_prompt = rmsnorm(xp, ln_final)
    y_sample = rmsnorm(xs, ln_final)
    rec_dtype = state_dn_recurrent.dtype
    return (y_prompt, y_sample,
            jnp.stack(p_conv), jnp.stack(p_rec).astype(rec_dtype), jnp.stack(p_k), jnp.stack(p_v),
            jnp.stack(s_conv), jnp.stack(s_rec).astype(rec_dtype), jnp.stack(s_k), jnp.stack(s_v))
```

```python
import functools
import math

import jax
import jax.numpy as jnp
from jax import lax
from jax.experimental import pallas as pl
from jax.experimental.pallas import tpu as pltpu

F32 = jnp.float32
BF16 = jnp.bfloat16
HIGHEST = lax.Precision.HIGHEST

D_MODEL = 4096
BATCH = 4
SEQ = 2048
DEPTH = 2
DEC_BATCH = 32
DEC_SEQ = 4
DN_HEADS = 16
DN_DK = 128
DN_DV = 128
DN_CONV = 4
DN_CHUNK = 64
DN_QK_W = DN_HEADS * DN_DK
DN_V_W = DN_HEADS * DN_DV
DN_CONV_DIM = 2 * DN_QK_W + DN_V_W
SW_HEADS = 32
SW_KV_HEADS = 8
SW_GROUP = SW_HEADS // SW_KV_HEADS
SW_HD = 64
SW_Q_W = SW_HEADS * SW_HD
SW_KV_W = SW_KV_HEADS * SW_HD
WINDOW = 128
D_FF = 11008
EPS = 1e-6

SUBLANES = 8
LANES = 128
VMEM_LIMIT = 56 * 1024 * 1024

DEC_PAD = SUBLANES
M_PROMPT = BATCH * SEQ
M_SAMPLE = DEC_BATCH * DEC_PAD
M_ROWS = M_PROMPT + M_SAMPLE
TM = M_ROWS // 8
TR = 256
TN = 1024

OFF_QKV = 0
OFF_Z = OFF_QKV + DN_CONV_DIM
OFF_SQ = OFF_Z + DN_V_W
OFF_SK = OFF_SQ + SW_Q_W
OFF_SV = OFF_SK + SW_KV_W
OFF_GA = OFF_SV + SW_KV_W
OFF_GB = OFF_GA + D_MODEL
IN_MAIN_W = OFF_GB + D_MODEL
ORIG_B = DN_CONV_DIM + DN_V_W
ORIG_A = ORIG_B + DN_HEADS
ORIG_SQ = ORIG_A + DN_HEADS

FF_TILE = 512
FF_PAD = -(-D_FF // TN) * TN
DOWN_TK = FF_PAD // 4
OUT_TK = D_MODEL // 2
MERGE_TN = 512

DN_HG = 4


def _cparams(sem):
    return pltpu.CompilerParams(dimension_semantics=sem, vmem_limit_bytes=VMEM_LIMIT)


def _rmsnorm_kernel(x_ref, w_ref, o_ref):
    x = x_ref[...]
    y = x * lax.rsqrt(jnp.mean(x * x, -1, keepdims=True) + EPS)
    o_ref[...] = (y * w_ref[...]).astype(o_ref.dtype)


def _rmsnorm(x, w, out_dtype):
    m, d = x.shape
    return pl.pallas_call(
        _rmsnorm_kernel,
        out_shape=jax.ShapeDtypeStruct((m, d), out_dtype),
        grid=(m // TR,),
        in_specs=[pl.BlockSpec((TR, d), lambda i: (i, 0)), pl.BlockSpec((1, d), lambda i: (0, 0))],
        out_specs=pl.BlockSpec((TR, d), lambda i: (i, 0)),
        compiler_params=_cparams(("parallel",)),
        name="rmsnorm",
    )(x, w.reshape(1, d))


def _mm_kernel(a_ref, w_ref, o_ref):
    o_ref[...] = jnp.dot(a_ref[...], w_ref[...], preferred_element_type=F32).astype(o_ref.dtype)


def _matmul(a, w, out_dtype, tn, name):
    m, k = a.shape
    n = w.shape[1]
    return pl.pallas_call(
        _mm_kernel,
        out_shape=jax.ShapeDtypeStruct((m, n), out_dtype),
        grid=(m // TM, n // tn),
        in_specs=[pl.BlockSpec((TM, k), lambda i, j: (i, 0)), pl.BlockSpec((k, tn), lambda i, j: (0, j))],
        out_specs=pl.BlockSpec((TM, tn), lambda i, j: (i, j)),
        compiler_params=_cparams(("parallel", "parallel")),
        name=name,
    )(a, w)


def _swiglu_kernel(a_ref, w_ref, o_ref):
    r = jnp.dot(a_ref[...], w_ref[...], preferred_element_type=F32)
    g = r[:, :FF_TILE]
    u = r[:, FF_TILE:]
    o_ref[...] = (g * jax.nn.sigmoid(g) * u).astype(o_ref.dtype)


def _swiglu_up(a, w_gu):
    m, k = a.shape
    nt = w_gu.shape[1] // (2 * FF_TILE)
    return pl.pallas_call(
        _swiglu_kernel,
        out_shape=jax.ShapeDtypeStruct((m, nt * FF_TILE), BF16),
        grid=(m // TM, nt),
        in_specs=[pl.BlockSpec((TM, k), lambda i, j: (i, 0)),
                  pl.BlockSpec((k, 2 * FF_TILE), lambda i, j: (0, j))],
        out_specs=pl.BlockSpec((TM, FF_TILE), lambda i, j: (i, j)),
        compiler_params=_cparams(("parallel", "parallel")),
        name="ffn_gate_up",
    )(a, w_gu)


def _residual_mm_kernel(a_ref, w_ref, r_ref, o_ref, *, scale):
    d = jnp.dot(a_ref[...], w_ref[...], preferred_element_type=F32)
    if scale != 1.0:
        d = d * scale

    @pl.when(pl.program_id(2) == 0)
    def _():
        o_ref[...] = r_ref[...] + d

    @pl.when(pl.program_id(2) != 0)
    def _():
        o_ref[...] += d


def _residual_matmul(a, w, res, scale, tk, name):
    m, k = a.shape
    n = w.shape[1]
    return pl.pallas_call(
        functools.partial(_residual_mm_kernel, scale=scale),
        out_shape=jax.ShapeDtypeStruct((m, n), F32),
        grid=(m // TM, n // TN, k // tk),
        in_specs=[pl.BlockSpec((TM, tk), lambda i, j, kk: (i, kk)),
                  pl.BlockSpec((tk, TN), lambda i, j, kk: (kk, j)),
                  pl.BlockSpec((TM, TN), lambda i, j, kk: (i, j))],
        out_specs=pl.BlockSpec((TM, TN), lambda i, j, kk: (i, j)),
        compiler_params=_cparams(("parallel", "parallel", "arbitrary")),
        name=name,
    )(a, w, res)


def _merge_kernel(oa_ref, ob_ref, wa_ref, wb_ref, ga_ref, gb_ref, o_ref):
    ya = jnp.dot(oa_ref[...], wa_ref[...], preferred_element_type=F32)
    yb = jnp.dot(ob_ref[...], wb_ref[...], preferred_element_type=F32)
    o = jax.nn.sigmoid(ga_ref[...]) * ya + jax.nn.sigmoid(gb_ref[...]) * yb
    o_ref[...] = o.astype(o_ref.dtype)


def _branch_merge(o_a, o_b, w_a, w_b, proj):
    m, ka = o_a.shape
    kb = o_b.shape[1]
    n = w_a.shape[1]
    tn = MERGE_TN
    ga_blk = OFF_GA // tn
    gb_blk = OFF_GB // tn
    return pl.pallas_call(
        _merge_kernel,
        out_shape=jax.ShapeDtypeStruct((m, n), BF16),
        grid=(m // TM, n // tn),
        in_specs=[pl.BlockSpec((TM, ka), lambda i, j: (i, 0)),
                  pl.BlockSpec((TM, kb), lambda i, j: (i, 0)),
                  pl.BlockSpec((ka, tn), lambda i, j: (0, j)),
                  pl.BlockSpec((kb, tn), lambda i, j: (0, j)),
                  pl.BlockSpec((TM, tn), lambda i, j: (i, ga_blk + j)),
                  pl.BlockSpec((TM, tn), lambda i, j: (i, gb_blk + j))],
        out_specs=pl.BlockSpec((TM, tn), lambda i, j: (i, j)),
        compiler_params=_cparams(("parallel", "parallel")),
        name="branch_merge",
    )(o_a, o_b, w_a, w_b, proj, proj)


def _dot_nt(a, b, precision=None):
    return lax.dot_general(a, b, (((1,), (1,)), ((), ())), precision=precision,
                           preferred_element_type=F32)


def _dot_tn(a, b):
    return lax.dot_general(a, b, (((0,), (0,)), ((), ())), preferred_element_type=F32)


def _dot_hi(a, b):
    return jnp.dot(a, b, precision=HIGHEST, preferred_element_type=F32)


def _unit_lower_inverse(a, c):
    ri = lax.broadcasted_iota(jnp.int32, (c, c), 0)
    ci = lax.broadcasted_iota(jnp.int32, (c, c), 1)
    eye = (ri == ci).astype(F32)
    base = min(16, c)
    same = (ri // base) == (ci // base)
    p = jnp.where(same, a, 0.0)
    t = eye - p
    for _ in range(int(math.log2(base)) - 1):
        p = _dot_hi(p, p)
        t = t + _dot_hi(t, p)
    blk = base
    while blk < c:
        pair = (ri // (2 * blk)) == (ci // (2 * blk))
        lower = pair & (((ri // blk) % 2) == 1) & (((ci // blk) % 2) == 0)
        l = jnp.where(lower, a, 0.0)
        t = t - _dot_hi(t, _dot_hi(l, t))
        blk *= 2
    return t


def _dn_kernel(qc_ref, kc_ref, vc_ref, qp_ref, kp_ref, vp_ref, qi_ref, ki_ref, vi_ref,
               z_ref, b_ref, a_ref, wq_ref, wk_ref, wv_ref, alog_ref, dtb_ref, nw_ref, s0_ref,
               o_ref, sout_ref, s_ref, ext_ref, gt_ref, *, c, t_valid, hg):
    hgrp = pl.program_id(1)
    ch = pl.program_id(2)
    nch = pl.num_programs(2)

    @pl.when(ch == 0)
    def _():
        s_ref[...] = s0_ref[0]

    def conv(cur_ref, prev_ref, init_ref, w_ref):
        prev = jnp.where(ch == 0, init_ref[0], prev_ref[...])
        ext_ref[0:SUBLANES, :] = prev
        ext_ref[SUBLANES:SUBLANES + c, :] = cur_ref[...]
        first = SUBLANES - (DN_CONV - 1)
        y = ext_ref[first:first + c, :] * w_ref[0:1, :]
        for j in range(1, DN_CONV):
            y = y + ext_ref[first + j:first + j + c, :] * w_ref[j:j + 1, :]
        return y * jax.nn.sigmoid(y)

    yq = conv(qc_ref, qp_ref, qi_ref, wq_ref)
    yk = conv(kc_ref, kp_ref, ki_ref, wk_ref)
    yv = conv(vc_ref, vp_ref, vi_ref, wv_ref)

    beta_all = jax.nn.sigmoid(b_ref[...])
    araw = a_ref[...] + dtb_ref[...]
    softplus = jnp.maximum(araw, 0.0) + jnp.log1p(jnp.exp(-jnp.abs(araw)))
    g_all = -jnp.exp(alog_ref[...]) * softplus
    if t_valid < c:
        row_ok = lax.broadcasted_iota(jnp.int32, (c, LANES), 0) < t_valid
        beta_all = jnp.where(row_ok, beta_all, 0.0)
        g_all = jnp.where(row_ok, g_all, 0.0)
    ri = lax.broadcasted_iota(jnp.int32, (c, c), 0)
    ci = lax.broadcasted_iota(jnp.int32, (c, c), 1)
    causal = ri >= ci
    strict = ri > ci
    g_cum = _dot_hi(causal.astype(F32), g_all)
    lri = lax.broadcasted_iota(jnp.int32, (LANES, LANES), 0)
    lci = lax.broadcasted_iota(jnp.int32, (LANES, LANES), 1)
    gt_ref[...] = _dot_nt((lri == lci).astype(F32), g_cum, HIGHEST)
    lane = lax.broadcasted_iota(jnp.int32, (c, LANES), 1)

    outs = []
    for h in range(hg):
        head = hgrp * hg + h
        sel = lane == head
        bcol = jnp.sum(jnp.where(sel, beta_all, 0.0), -1, keepdims=True)
        gcol = jnp.sum(jnp.where(sel, g_cum, 0.0), -1, keepdims=True)
        grow = gt_ref[pl.ds(head, 1), :]
        glast = grow[:, c - 1:c]
        sl = slice(h * DN_DK, (h + 1) * DN_DK)
        qh = yq[:, sl]
        kh = yk[:, sl]
        vh = yv[:, sl]
        qh = qh * lax.rsqrt(jnp.sum(qh * qh, -1, keepdims=True) + 1e-6) * (DN_DK ** -0.5)
        kh = kh * lax.rsqrt(jnp.sum(kh * kh, -1, keepdims=True) + 1e-6)
        decay = jnp.exp(jnp.where(causal, gcol - grow, -jnp.inf))
        kb = kh * bcol
        khb = kh.astype(BF16)
        a_mat = jnp.where(strict, _dot_nt(kb.astype(BF16), khb) * decay, 0.0)
        t_inv = _unit_lower_inverse(a_mat, c)
        eg = jnp.exp(gcol)
        rhs = jnp.concatenate([vh * bcol, kb * eg], axis=1)
        sol = _dot_hi(t_inv, rhs)
        u_base = sol[:, :DN_DV]
        w_cum = sol[:, DN_DV:]
        a_qk = _dot_nt(qh.astype(BF16), khb) * decay
        q_dec = qh * eg
        k_dec = kh * jnp.exp(glast - gcol)
        s_old = s_ref[h]
        s_b = s_old.astype(BF16)
        u = u_base - jnp.dot(w_cum.astype(BF16), s_b, preferred_element_type=F32)
        u_b = u.astype(BF16)
        o = (jnp.dot(q_dec.astype(BF16), s_b, preferred_element_type=F32)
             + jnp.dot(a_qk.astype(BF16), u_b, preferred_element_type=F32))
        s_ref[h] = s_old * jnp.exp(glast) + _dot_tn(k_dec.astype(BF16), u_b)
        o = o * lax.rsqrt(jnp.mean(o * o, -1, keepdims=True) + EPS) * nw_ref[...]
        zh = z_ref[:, sl]
        outs.append(o * (zh * jax.nn.sigmoid(zh)))
    o_ref[...] = jnp.concatenate(outs, axis=1).astype(o_ref.dtype)

    @pl.when(ch == nch - 1)
    def _():
        sout_ref[0] = s_ref[...]


def _deltanet(proj, gates, conv_init, s0, conv_w, a_log, dt_bias, norm_w, *, nb, t, c, t_valid, row0):
    hg = DN_HG
    w = hg * DN_DK
    nch = t // c
    cur_row = lambda b, g, k: row0 // c + b * nch + k
    prev_row = lambda b, g, k: jnp.maximum(row0 // SUBLANES + (b * t + k * c) // SUBLANES - 1, 0)
    kq, kk, kv, kz = (OFF_QKV // w, (OFF_QKV + DN_QK_W) // w, (OFF_QKV + 2 * DN_QK_W) // w, OFF_Z // w)

    def cur(col0):
        return pl.BlockSpec((c, w), lambda b, g, k: (cur_row(b, g, k), col0 + g))

    def prev(col0):
        return pl.BlockSpec((SUBLANES, w), lambda b, g, k: (prev_row(b, g, k), col0 + g))

    def init(col0):
        return pl.BlockSpec((1, SUBLANES, w), lambda b, g, k: (b, 0, col0 + g))

    def taps(col0):
        return pl.BlockSpec((DN_CONV, w), lambda b, g, k: (0, col0 + g))

    row = lambda: pl.BlockSpec((1, LANES), lambda b, g, k: (0, 0))
    pad = lambda v: jnp.zeros((1, LANES), F32).at[0, :DN_HEADS].set(v.astype(F32))
    state = pl.BlockSpec((1, hg, DN_DK, DN_DV), lambda b, g, k: (b, g, 0, 0))
    o, s_new = pl.pallas_call(
        functools.partial(_dn_kernel, c=c, t_valid=t_valid, hg=hg),
        out_shape=(jax.ShapeDtypeStruct((nb * t, DN_V_W), BF16),
                   jax.ShapeDtypeStruct((nb, DN_HEADS, DN_DK, DN_DV), F32)),
        grid=(nb, DN_HEADS // hg, nch),
        in_specs=[cur(kq), cur(kk), cur(kv), prev(kq), prev(kk), prev(kv),
                  init(kq), init(kk), init(kv), cur(kz),
                  pl.BlockSpec((c, LANES), lambda b, g, k: (cur_row(b, g, k), 0)),
                  pl.BlockSpec((c, LANES), lambda b, g, k: (cur_row(b, g, k), 1)),
                  taps(kq), taps(kk), taps(kv), row(), row(), row(), state],
        out_specs=(pl.BlockSpec((c, w), lambda b, g, k: (b * nch + k, g)), state),
        scratch_shapes=[pltpu.VMEM((hg, DN_DK, DN_DV), F32),
                        pltpu.VMEM((c + SUBLANES, w), F32),
                        pltpu.VMEM((LANES, c), F32)],
        compiler_params=_cparams(("parallel", "parallel", "arbitrary")),
        name="deltanet",
    )(proj, proj, proj, proj, proj, proj, conv_init, conv_init, conv_init, proj, gates, gates,
      conv_w, conv_w, conv_w, pad(a_log), pad(dt_bias), norm_w.reshape(1, DN_DV).astype(F32), s0)
    return o, s_new


def _swa_kernel(slopes_ref, sinks_ref, q_ref, kp_ref, kc_ref, vp_ref, vc_ref, o_ref, *, tq, first_prev_valid):
    blk = pl.program_id(1)
    kvp = pl.program_id(2)
    tk = WINDOW + tq
    q = q_ref[...]
    k = jnp.concatenate([kp_ref[...], kc_ref[...]], axis=0).astype(BF16)
    v = jnp.concatenate([vp_ref[...], vc_ref[...]], axis=0).astype(BF16)
    ri = lax.broadcasted_iota(jnp.int32, (tq, tk), 0)
    ci = lax.broadcasted_iota(jnp.int32, (tq, tk), 1)
    dist = ri + WINDOW - ci
    valid = (dist >= 0) & (dist <= WINDOW)
    if not first_prev_valid:
        valid = valid & ((ci >= WINDOW) | (blk > 0))
    distf = dist.astype(F32)
    heads_per_step = 2 * SW_GROUP
    outs = []
    for kh in range(2):
        kk = k[:, kh * SW_HD:(kh + 1) * SW_HD]
        vv = v[:, kh * SW_HD:(kh + 1) * SW_HD]
        for g in range(SW_GROUP):
            hl = kh * SW_GROUP + g
            head = kvp * heads_per_step + hl
            qg = q[:, hl * SW_HD:(hl + 1) * SW_HD].astype(BF16)
            s = _dot_nt(qg, kk) * (SW_HD ** -0.5)
            s = jnp.where(valid, s - slopes_ref[head] * distf, -jnp.inf)
            sink = sinks_ref[head]
            m = jnp.maximum(jnp.max(s, -1, keepdims=True), sink)
            p = jnp.exp(s - m)
            p = p / (jnp.sum(p, -1, keepdims=True) + jnp.exp(sink - m))
            outs.append(jnp.dot(p.astype(BF16), vv, preferred_element_type=F32))
    o_ref[...] = jnp.concatenate(outs, axis=1).astype(o_ref.dtype)


def _swa(proj, k_prev_src, v_prev_src, slopes, sinks, *, nb, nblk, tq, row0, prev_from_proj):
    qw = 2 * SW_GROUP * SW_HD
    kw = 2 * SW_HD
    qb, kb, vb = OFF_SQ // qw, OFF_SK // kw, OFF_SV // kw
    cur_row = lambda b, n, p: row0 // tq + b * nblk + n
    if prev_from_proj:
        prev_k = pl.BlockSpec((WINDOW, kw), lambda b, n, p: (b * nblk + jnp.maximum(n - 1, 0), kb + p))
        prev_v = pl.BlockSpec((WINDOW, kw), lambda b, n, p: (b * nblk + jnp.maximum(n - 1, 0), vb + p))
    else:
        prev_k = pl.BlockSpec((WINDOW, kw), lambda b, n, p: (b, p))
        prev_v = pl.BlockSpec((WINDOW, kw), lambda b, n, p: (b, p))
    smem = pl.BlockSpec(memory_space=pltpu.SMEM)
    return pl.pallas_call(
        functools.partial(_swa_kernel, tq=tq, first_prev_valid=not prev_from_proj),
        out_shape=jax.ShapeDtypeStruct((nb * nblk * tq, SW_Q_W), BF16),
        grid=(nb, nblk, SW_KV_HEADS // 2),
        in_specs=[smem, smem,
                  pl.BlockSpec((tq, qw), lambda b, n, p: (cur_row(b, n, p), qb + p)),
                  prev_k,
                  pl.BlockSpec((tq, kw), lambda b, n, p: (cur_row(b, n, p), kb + p)),
                  prev_v,
                  pl.BlockSpec((tq, kw), lambda b, n, p: (cur_row(b, n, p), vb + p))],
        out_specs=pl.BlockSpec((tq, qw), lambda b, n, p: (b * nblk + n, p)),
        compiler_params=_cparams(("parallel", "parallel", "parallel")),
        name="swa",
    )(slopes, sinks, proj, k_prev_src, proj, v_prev_src, proj)


def _prep_in_proj(w_in):
    main = jnp.concatenate([w_in[:, :ORIG_B], w_in[:, ORIG_SQ:]], axis=1).astype(BF16)
    side = jnp.zeros((w_in.shape[0], 2 * LANES), BF16)
    side = side.at[:, :DN_HEADS].set(w_in[:, ORIG_B:ORIG_A].astype(BF16))
    side = side.at[:, LANES:LANES + DN_HEADS].set(w_in[:, ORIG_A:ORIG_SQ].astype(BF16))
    return main, side


def _prep_ffn(wg, wu, wd):
    d = wg.shape[0]
    padc = ((0, 0), (0, FF_PAD - D_FF))
    g = jnp.pad(wg, padc).astype(BF16).reshape(d, FF_PAD // FF_TILE, 1, FF_TILE)
    u = jnp.pad(wu, padc).astype(BF16).reshape(d, FF_PAD // FF_TILE, 1, FF_TILE)
    w_gu = jnp.concatenate([g, u], axis=2).reshape(d, 2 * FF_PAD)
    w_d = jnp.pad(wd, ((0, FF_PAD - D_FF), (0, 0))).astype(BF16)
    return w_gu, w_d


def _ffn_half_step(x, ln, w_gu, w_d, name):
    h = _rmsnorm(x, ln, BF16)
    hidden = _swiglu_up(h, w_gu)
    return _residual_matmul(hidden, w_d, x, 0.5, DOWN_TK, name)


def kernel(x_prompt, x_sample, state_dn_conv, state_dn_recurrent, cache_swa_k, cache_swa_v, ln_ffn1, w_ffn1_gate, w_ffn1_up, w_ffn1_down, ln_mix, w_in, dn_conv_w, dn_a_log, dn_dt_bias, dn_norm_w, swa_sinks, w_branch_a, w_branch_b, w_out, ln_ffn2, w_ffn2_gate, w_ffn2_up, w_ffn2_down, ln_final):
    slopes = 2.0 ** (-8.0 * jnp.arange(1, SW_HEADS + 1, dtype=F32) / SW_HEADS)
    xs = jnp.pad(x_sample, ((0, 0), (0, DEC_PAD - DEC_SEQ), (0, 0)))
    x = jnp.concatenate([x_prompt.reshape(M_PROMPT, D_MODEL), xs.reshape(M_SAMPLE, D_MODEL)], axis=0)
    zero_conv = jnp.zeros((BATCH, SUBLANES, DN_CONV_DIM), F32)
    zero_state = jnp.zeros((BATCH, DN_HEADS, DN_DK, DN_DV), F32)
    outs = [[] for _ in range(8)]
    for l in range(DEPTH):
        w_gu1, w_d1 = _prep_ffn(w_ffn1_gate[l], w_ffn1_up[l], w_ffn1_down[l])
        w_gu2, w_d2 = _prep_ffn(w_ffn2_gate[l], w_ffn2_up[l], w_ffn2_down[l])
        w_main, w_side = _prep_in_proj(w_in[l])

        x = _ffn_half_step(x, ln_ffn1[l], w_gu1, w_d1, "ffn1_down")

        h = _rmsnorm(x, ln_mix[l], BF16)
        proj = _matmul(h, w_main, F32, TN, "in_proj")
        gates = _matmul(h, w_side, F32, 2 * LANES, "in_proj_gates")

        conv_s = jnp.pad(state_dn_conv[l], ((0, 0), (SUBLANES - (DN_CONV - 1), 0), (0, 0)))
        dn_args = (dn_conv_w[l], dn_a_log[l], dn_dt_bias[l], dn_norm_w[l])
        oa_p, rec_p = _deltanet(proj, gates, zero_conv, zero_state, *dn_args,
                                nb=BATCH, t=SEQ, c=DN_CHUNK, t_valid=DN_CHUNK, row0=0)
        oa_s, rec_s = _deltanet(proj, gates, conv_s, state_dn_recurrent[l].astype(F32), *dn_args,
                                nb=DEC_BATCH, t=DEC_PAD, c=DEC_PAD, t_valid=DEC_SEQ, row0=M_PROMPT)

        buf = cache_swa_k.shape[2]
        ck = cache_swa_k[l].reshape(DEC_BATCH * buf, SW_KV_W)
        cv = cache_swa_v[l].reshape(DEC_BATCH * buf, SW_KV_W)
        ob_p = _swa(proj, proj, proj, slopes, swa_sinks[l].astype(F32), nb=BATCH, nblk=SEQ // WINDOW,
                    tq=WINDOW, row0=0, prev_from_proj=True)
        ob_s = _swa(proj, ck, cv, slopes, swa_sinks[l].astype(F32), nb=DEC_BATCH, nblk=1,
                    tq=DEC_PAD, row0=M_PROMPT, prev_from_proj=False)

        o_a = jnp.concatenate([oa_p, oa_s], axis=0)
        o_b = jnp.concatenate([ob_p, ob_s], axis=0)
        merged = _branch_merge(o_a, o_b, w_branch_a[l].astype(BF16), w_branch_b[l].astype(BF16), proj)
        x = _residual_matmul(merged, w_out[l].astype(BF16), x, 1.0, OUT_TK, "out_proj")

        x = _ffn_half_step(x, ln_ffn2[l], w_gu2, w_d2, "ffn2_down")

        pp = proj[:M_PROMPT].reshape(BATCH, SEQ, IN_MAIN_W)
        ps = proj[M_PROMPT:].reshape(DEC_BATCH, DEC_PAD, IN_MAIN_W)[:, :DEC_SEQ]
        keep = min(WINDOW, SEQ)
        outs[0].append(pp[:, SEQ - (DN_CONV - 1):, OFF_QKV:OFF_QKV + DN_CONV_DIM])
        outs[1].append(rec_p)
        outs[2].append(pp[:, SEQ - keep:, OFF_SK:OFF_SK + SW_KV_W].reshape(BATCH, keep, SW_KV_HEADS, SW_HD))
        outs[3].append(pp[:, SEQ - keep:, OFF_SV:OFF_SV + SW_KV_W].reshape(BATCH, keep, SW_KV_HEADS, SW_HD))
        outs[4].append(ps[:, DEC_SEQ - (DN_CONV - 1):, OFF_QKV:OFF_QKV + DN_CONV_DIM])
        outs[5].append(rec_s)
        new_k = ps[:, :, OFF_SK:OFF_SK + SW_KV_W].reshape(DEC_BATCH, DEC_SEQ, SW_KV_HEADS, SW_HD)
        new_v = ps[:, :, OFF_SV:OFF_SV + SW_KV_W].reshape(DEC_BATCH, DEC_SEQ, SW_KV_HEADS, SW_HD)
        outs[6].append(jnp.concatenate([cache_swa_k[l], new_k], axis=1)[:, -buf:])
        outs[7].append(jnp.concatenate([cache_swa_v[l], new_v], axis=1)[:, -buf:])

    y = _rmsnorm(x, ln_final, F32)
    y_prompt = y[:M_PROMPT].reshape(BATCH, SEQ, D_MODEL)
    y_sample = y[M_PROMPT:].reshape(DEC_BATCH, DEC_PAD, D_MODEL)[:, :DEC_SEQ]
    rec_dtype = state_dn_recurrent.dtype
    return (y_prompt, y_sample,
            jnp.stack(outs[0]), jnp.stack(outs[1]).astype(rec_dtype), jnp.stack(outs[2]), jnp.stack(outs[3]),
            jnp.stack(outs[4]), jnp.stack(outs[5]).astype(rec_dtype), jnp.stack(outs[6]), jnp.stack(outs[7]))
```

```python
import functools
import math

import jax
import jax.numpy as jnp
from jax import lax
from jax.experimental import pallas as pl
from jax.experimental.pallas import tpu as pltpu

F32 = jnp.float32
BF16 = jnp.bfloat16
HIGHEST = lax.Precision.HIGHEST

D_MODEL = 4096
BATCH = 4
SEQ = 2048
DEPTH = 2
DEC_BATCH = 32
DEC_SEQ = 4
DN_HEADS = 16
DN_DK = 128
DN_DV = 128
DN_CONV = 4
DN_CHUNK = 64
DN_QK_W = DN_HEADS * DN_DK
DN_V_W = DN_HEADS * DN_DV
DN_CONV_DIM = 2 * DN_QK_W + DN_V_W
SW_HEADS = 32
SW_KV_HEADS = 8
SW_GROUP = SW_HEADS // SW_KV_HEADS
SW_HD = 64
SW_Q_W = SW_HEADS * SW_HD
SW_KV_W = SW_KV_HEADS * SW_HD
WINDOW = 128
D_FF = 11008
EPS = 1e-6

SUBLANES = 8
LANES = 128
VMEM_LIMIT = 56 * 1024 * 1024

DEC_PAD = SUBLANES
M_PROMPT = BATCH * SEQ
M_SAMPLE = DEC_BATCH * DEC_PAD
M_ROWS = M_PROMPT + M_SAMPLE
TM = M_ROWS // 8
TR = 256
TN = 1024
TN_HALF = 512

OFF_QKV = 0
OFF_Z = OFF_QKV + DN_CONV_DIM
OFF_SQ = OFF_Z + DN_V_W
OFF_SK = OFF_SQ + SW_Q_W
OFF_SV = OFF_SK + SW_KV_W
OFF_GA = OFF_SV + SW_KV_W
OFF_GB = OFF_GA + D_MODEL
IN_MAIN_W = OFF_GB + D_MODEL
ORIG_B = DN_CONV_DIM + DN_V_W
ORIG_A = ORIG_B + DN_HEADS
ORIG_SQ = ORIG_A + DN_HEADS

DOWN_TK = D_FF // 2
OUT_TK = D_MODEL // 2

DN_STACK = 4
SW_PAIR = 2


def _cparams(sem):
    return pltpu.CompilerParams(dimension_semantics=sem, vmem_limit_bytes=VMEM_LIMIT)


def _rmsnorm_kernel(x_ref, w_ref, o_ref):
    x = x_ref[...]
    y = x * lax.rsqrt(jnp.mean(x * x, -1, keepdims=True) + EPS)
    o_ref[...] = (y * w_ref[...]).astype(o_ref.dtype)


def _rmsnorm(x, w, out_dtype, row0=0, nrows=None):
    d = x.shape[1]
    nrows = x.shape[0] if nrows is None else nrows
    blk0 = row0 // TR
    return pl.pallas_call(
        _rmsnorm_kernel,
        out_shape=jax.ShapeDtypeStruct((nrows, d), out_dtype),
        grid=(nrows // TR,),
        in_specs=[pl.BlockSpec((TR, d), lambda i: (blk0 + i, 0)), pl.BlockSpec((1, d), lambda i: (0, 0))],
        out_specs=pl.BlockSpec((TR, d), lambda i: (i, 0)),
        compiler_params=_cparams(("parallel",)),
        name="rmsnorm",
    )(x, w.reshape(1, d))


def _mm_kernel(a_ref, w_ref, o_ref):
    o_ref[...] = jnp.dot(a_ref[...], w_ref[...], preferred_element_type=F32).astype(o_ref.dtype)


def _matmul(a, w, layer, out_dtype, tn, name):
    m, k = a.shape
    n = w.shape[2]
    return pl.pallas_call(
        _mm_kernel,
        out_shape=jax.ShapeDtypeStruct((m, n), out_dtype),
        grid=(m // TM, n // tn),
        in_specs=[pl.BlockSpec((TM, k), lambda i, j: (i, 0)),
                  pl.BlockSpec((None, k, tn), lambda i, j: (layer, 0, j))],
        out_specs=pl.BlockSpec((TM, tn), lambda i, j: (i, j)),
        compiler_params=_cparams(("parallel", "parallel")),
        name=name,
    )(a, w)


def _swiglu_kernel(a_ref, wg_ref, wu_ref, o_ref):
    a = a_ref[...]
    g = jnp.dot(a, wg_ref[...], preferred_element_type=F32)
    u = jnp.dot(a, wu_ref[...], preferred_element_type=F32)
    o_ref[...] = (g * jax.nn.sigmoid(g) * u).astype(o_ref.dtype)


def _swiglu_up(a, wg, wu, layer):
    m, k = a.shape
    n = wg.shape[2]
    tn = TN_HALF
    wspec = pl.BlockSpec((None, k, tn), lambda i, j: (layer, 0, j))
    return pl.pallas_call(
        _swiglu_kernel,
        out_shape=jax.ShapeDtypeStruct((m, n), BF16),
        grid=(m // TM, pl.cdiv(n, tn)),
        in_specs=[pl.BlockSpec((TM, k), lambda i, j: (i, 0)), wspec, wspec],
        out_specs=pl.BlockSpec((TM, tn), lambda i, j: (i, j)),
        compiler_params=_cparams(("parallel", "parallel")),
        name="ffn_gate_up",
    )(a, wg, wu)


def _residual_mm_kernel(a_ref, w_ref, r_ref, o_ref, *, scale):
    d = jnp.dot(a_ref[...], w_ref[...], preferred_element_type=F32)
    if scale != 1.0:
        d = d * scale

    @pl.when(pl.program_id(2) == 0)
    def _():
        o_ref[...] = r_ref[...] + d

    @pl.when(pl.program_id(2) != 0)
    def _():
        o_ref[...] += d


def _residual_matmul(a, w, layer, res, scale, tk, tn, name):
    m, k = a.shape
    n = w.shape[2]
    return pl.pallas_call(
        functools.partial(_residual_mm_kernel, scale=scale),
        out_shape=jax.ShapeDtypeStruct((m, n), F32),
        grid=(m // TM, n // tn, k // tk),
        in_specs=[pl.BlockSpec((TM, tk), lambda i, j, kk: (i, kk)),
                  pl.BlockSpec((None, tk, tn), lambda i, j, kk: (layer, kk, j)),
                  pl.BlockSpec((TM, tn), lambda i, j, kk: (i, j))],
        out_specs=pl.BlockSpec((TM, tn), lambda i, j, kk: (i, j)),
        compiler_params=_cparams(("parallel", "parallel", "arbitrary")),
        name=name,
    )(a, w, res)


def _merge_kernel(oa_ref, ob_ref, wa_ref, wb_ref, ga_ref, gb_ref, o_ref):
    ya = jnp.dot(oa_ref[...], wa_ref[...], preferred_element_type=F32)
    yb = jnp.dot(ob_ref[...], wb_ref[...], preferred_element_type=F32)
    o = jax.nn.sigmoid(ga_ref[...]) * ya + jax.nn.sigmoid(gb_ref[...]) * yb
    o_ref[...] = o.astype(o_ref.dtype)


def _branch_merge(o_a, o_b, w_a, w_b, layer, proj):
    m, ka = o_a.shape
    kb = o_b.shape[1]
    n = w_a.shape[2]
    tn = TN_HALF
    ga_blk = OFF_GA // tn
    gb_blk = OFF_GB // tn
    return pl.pallas_call(
        _merge_kernel,
        out_shape=jax.ShapeDtypeStruct((m, n), BF16),
        grid=(m // TM, n // tn),
        in_specs=[pl.BlockSpec((TM, ka), lambda i, j: (i, 0)),
                  pl.BlockSpec((TM, kb), lambda i, j: (i, 0)),
                  pl.BlockSpec((None, ka, tn), lambda i, j: (layer, 0, j)),
                  pl.BlockSpec((None, kb, tn), lambda i, j: (layer, 0, j)),
                  pl.BlockSpec((TM, tn), lambda i, j: (i, ga_blk + j)),
                  pl.BlockSpec((TM, tn), lambda i, j: (i, gb_blk + j))],
        out_specs=pl.BlockSpec((TM, tn), lambda i, j: (i, j)),
        compiler_params=_cparams(("parallel", "parallel")),
        name="branch_merge",
    )(o_a, o_b, w_a, w_b, proj, proj)


def _dot_nt(a, b, precision=None):
    return lax.dot_general(a, b, (((1,), (1,)), ((), ())), precision=precision,
                           preferred_element_type=F32)


def _dot_tn(a, b):
    return lax.dot_general(a, b, (((0,), (0,)), ((), ())), preferred_element_type=F32)


def _dot_hi(a, b):
    return jnp.dot(a, b, precision=HIGHEST, preferred_element_type=F32)


def _split(x):
    hi = x.astype(BF16)
    return hi, (x - hi.astype(F32)).astype(BF16)


def _dot_3x(a, b):
    return (jnp.dot(a[0], b[0], preferred_element_type=F32)
            + jnp.dot(a[0], b[1], preferred_element_type=F32)
            + jnp.dot(a[1], b[0], preferred_element_type=F32))


def _unit_lower_inverse(a, n, c):
    ri = lax.broadcasted_iota(jnp.int32, (n, n), 0)
    ci = lax.broadcasted_iota(jnp.int32, (n, n), 1)
    eye = (ri == ci).astype(F32)
    base = min(16, c)
    p = jnp.where((ri // base) == (ci // base), a, 0.0)
    t = eye - p
    for _ in range(int(math.log2(base)) - 1):
        ps = _split(p)
        p = _dot_3x(ps, ps)
        t = t + _dot_3x(_split(t), _split(p))
    blk = base
    while blk < c:
        pair = (ri // (2 * blk)) == (ci // (2 * blk))
        lower = pair & (((ri // blk) % 2) == 1) & (((ci // blk) % 2) == 0)
        ts = _split(t)
        lt = _dot_3x(_split(jnp.where(lower, a, 0.0)), ts)
        t = t - _dot_3x(ts, _split(lt))
        blk *= 2
    return t


def _dn_kernel(qc_ref, kc_ref, vc_ref, qp_ref, kp_ref, vp_ref, qi_ref, ki_ref, vi_ref,
               z_ref, b_ref, a_ref, wq_ref, wk_ref, wv_ref, alog_ref, dtb_ref, nw_ref, s0_ref,
               o_ref, sout_ref, s_ref, ext_ref, *, c, t_valid):
    ch = pl.program_id(1)
    nch = pl.num_programs(1)
    n = DN_STACK * c

    @pl.when(ch == 0)
    def _():
        s_ref[...] = s0_ref[0]

    def conv(cur_ref, prev_ref, init_ref, w_ref):
        prev = jnp.where(ch == 0, init_ref[0], prev_ref[...])
        ext_ref[0:SUBLANES, :] = prev
        ext_ref[SUBLANES:SUBLANES + c, :] = cur_ref[...]
        first = SUBLANES - (DN_CONV - 1)
        y = ext_ref[first:first + c, :] * w_ref[0:1, :]
        for j in range(1, DN_CONV):
            y = y + ext_ref[first + j:first + j + c, :] * w_ref[j:j + 1, :]
        return y * jax.nn.sigmoid(y)

    yq = conv(qc_ref, qp_ref, qi_ref, wq_ref)
    yk = conv(kc_ref, kp_ref, ki_ref, wk_ref)
    yv = conv(vc_ref, vp_ref, vi_ref, wv_ref)
    zz = z_ref[...]

    beta_all = jax.nn.sigmoid(b_ref[...])
    araw = a_ref[...] + dtb_ref[...]
    softplus = jnp.maximum(araw, 0.0) + jnp.log1p(jnp.exp(-jnp.abs(araw)))
    g_all = -jnp.exp(alog_ref[...]) * softplus
    if t_valid < c:
        row_ok = lax.broadcasted_iota(jnp.int32, (c, LANES), 0) < t_valid
        beta_all = jnp.where(row_ok, beta_all, 0.0)
        g_all = jnp.where(row_ok, g_all, 0.0)
    tri = lax.broadcasted_iota(jnp.int32, (c, c), 0) >= lax.broadcasted_iota(jnp.int32, (c, c), 1)
    g_cum = _dot_hi(tri.astype(F32), g_all)
    lri = lax.broadcasted_iota(jnp.int32, (LANES, LANES), 0)
    lci = lax.broadcasted_iota(jnp.int32, (LANES, LANES), 1)
    g_cum_t = _dot_nt((lri == lci).astype(F32), g_cum, HIGHEST)

    ri = lax.broadcasted_iota(jnp.int32, (n, n), 0)
    ci = lax.broadcasted_iota(jnp.int32, (n, n), 1)
    same_head = (ri // c) == (ci // c)
    causal = same_head & ((ri % c) >= (ci % c))
    strict = same_head & ((ri % c) > (ci % c))

    for grp in range(DN_HEADS // DN_STACK):
        heads = [grp * DN_STACK + j for j in range(DN_STACK)]

        def stack(y):
            return jnp.concatenate([y[:, h * DN_DK:(h + 1) * DN_DK] for h in heads], axis=0)

        def column(m):
            return jnp.concatenate([m[:, h:h + 1] for h in heads], axis=0)

        q = stack(yq)
        k = stack(yk)
        v = stack(yv)
        bcol = column(beta_all)
        gcol = column(g_cum)
        grow = jnp.concatenate([g_cum_t[h:h + 1, :] for h in heads], axis=1)
        glast = jnp.concatenate([jnp.broadcast_to(g_cum[c - 1:c, h:h + 1], (c, 1)) for h in heads], axis=0)
        q = q * lax.rsqrt(jnp.sum(q * q, -1, keepdims=True) + 1e-6) * (DN_DK ** -0.5)
        k = k * lax.rsqrt(jnp.sum(k * k, -1, keepdims=True) + 1e-6)
        decay = jnp.exp(jnp.where(causal, gcol - grow, -jnp.inf))
        kb = k * bcol
        k_b = k.astype(BF16)
        a_mat = jnp.where(strict, _dot_nt(kb.astype(BF16), k_b) * decay, 0.0)
        t_inv = _unit_lower_inverse(a_mat, n, c)
        eg = jnp.exp(gcol)
        rhs = jnp.concatenate([v * bcol, kb * eg], axis=1)
        sol = _dot_3x(_split(t_inv), _split(rhs))
        u_base = sol[:, :DN_DV]
        w_cum = sol[:, DN_DV:].astype(BF16)
        a_qk = (_dot_nt(q.astype(BF16), k_b) * decay).astype(BF16)
        q_dec = (q * eg).astype(BF16)
        k_dec = (k * jnp.exp(glast - gcol)).astype(BF16)

        s_old = [s_ref[h] for h in heads]
        ws, qs = [], []
        for j in range(DN_STACK):
            rows = slice(j * c, (j + 1) * c)
            lhs = jnp.concatenate([w_cum[rows], q_dec[rows]], axis=0)
            r = jnp.dot(lhs, s_old[j].astype(BF16), preferred_element_type=F32)
            ws.append(r[:c])
            qs.append(r[c:])
        u = u_base - jnp.concatenate(ws, axis=0)
        u_b = u.astype(BF16)
        o = jnp.concatenate(qs, axis=0) + jnp.dot(a_qk, u_b, preferred_element_type=F32)
        for j, h in enumerate(heads):
            rows = slice(j * c, (j + 1) * c)
            g_tot = jnp.exp(g_cum[c - 1:c, h:h + 1])
            s_ref[h] = s_old[j] * g_tot + _dot_tn(k_dec[rows], u_b[rows])
        o = o * lax.rsqrt(jnp.mean(o * o, -1, keepdims=True) + EPS) * nw_ref[...]
        z = stack(zz)
        o = o * (z * jax.nn.sigmoid(z))
        for j, h in enumerate(heads):
            o_ref[:, h * DN_DV:(h + 1) * DN_DV] = o[j * c:(j + 1) * c].astype(o_ref.dtype)

    @pl.when(ch == nch - 1)
    def _():
        sout_ref[0] = s_ref[...]


def _deltanet(proj, gates, conv_init, s0, conv_w, a_log, dt_bias, norm_w, *, nb, t, c, t_valid, row0):
    w = DN_QK_W
    nch = t // c
    cur_row = lambda b, k: row0 // c + b * nch + k
    prev_row = lambda b, k: jnp.maximum(row0 // SUBLANES + (b * t + k * c) // SUBLANES - 1, 0)
    kq, kk, kv, kz = (OFF_QKV // w, (OFF_QKV + DN_QK_W) // w, (OFF_QKV + 2 * DN_QK_W) // w, OFF_Z // w)

    def cur(col):
        return pl.BlockSpec((c, w), lambda b, k: (cur_row(b, k), col))

    def prev(col):
        return pl.BlockSpec((SUBLANES, w), lambda b, k: (prev_row(b, k), col))

    def init(col):
        return pl.BlockSpec((1, SUBLANES, w), lambda b, k: (b, 0, col))

    def taps(col):
        return pl.BlockSpec((DN_CONV, w), lambda b, k: (0, col))

    row = lambda: pl.BlockSpec((1, LANES), lambda b, k: (0, 0))
    pad = lambda x: jnp.zeros((1, LANES), F32).at[0, :DN_HEADS].set(x.astype(F32))
    state = pl.BlockSpec((1, DN_HEADS, DN_DK, DN_DV), lambda b, k: (b, 0, 0, 0))
    o, s_new = pl.pallas_call(
        functools.partial(_dn_kernel, c=c, t_valid=t_valid),
        out_shape=(jax.ShapeDtypeStruct((nb * t, DN_V_W), BF16),
                   jax.ShapeDtypeStruct((nb, DN_HEADS, DN_DK, DN_DV), F32)),
        grid=(nb, nch),
        in_specs=[cur(kq), cur(kk), cur(kv), prev(kq), prev(kk), prev(kv),
                  init(kq), init(kk), init(kv), cur(kz),
                  pl.BlockSpec((c, LANES), lambda b, k: (cur_row(b, k), 0)),
                  pl.BlockSpec((c, LANES), lambda b, k: (cur_row(b, k), 1)),
                  taps(kq), taps(kk), taps(kv), row(), row(), row(), state],
        out_specs=(pl.BlockSpec((c, w), lambda b, k: (b * nch + k, 0)), state),
        scratch_shapes=[pltpu.VMEM((DN_HEADS, DN_DK, DN_DV), F32),
                        pltpu.VMEM((c + SUBLANES, w), F32)],
        compiler_params=_cparams(("parallel", "arbitrary")),
        name="deltanet",
    )(proj, proj, proj, proj, proj, proj, conv_init, conv_init, conv_init, proj, gates, gates,
      conv_w, conv_w, conv_w, pad(a_log), pad(dt_bias), norm_w.reshape(1, DN_DV).astype(F32), s0)
    return o, s_new


def _swa_kernel(slopes_ref, sinks_ref, q_ref, kp_ref, kc_ref, vp_ref, vc_ref, o_ref, *, tq, first_prev_valid):
    blk = pl.program_id(1)
    pair = pl.program_id(2)
    tk = WINDOW + tq
    n = SW_GROUP * tq
    k = jnp.concatenate([kp_ref[...], kc_ref[...]], axis=0).astype(BF16)
    v = jnp.concatenate([vp_ref[...], vc_ref[...]], axis=0).astype(BF16)
    q = q_ref[...]
    q = jnp.concatenate([q[:, g * LANES:(g + 1) * LANES] for g in range(SW_GROUP)], axis=0)
    low = lax.broadcasted_iota(jnp.int32, (n, LANES), 1) < SW_HD
    ri = lax.broadcasted_iota(jnp.int32, (n, tk), 0)
    ci = lax.broadcasted_iota(jnp.int32, (n, tk), 1)
    dist = (ri % tq) + WINDOW - ci
    valid = (dist >= 0) & (dist <= WINDOW)
    if not first_prev_valid:
        valid = valid & ((ci >= WINDOW) | (blk > 0))
    distf = dist.astype(F32)
    grp = lax.broadcasted_iota(jnp.int32, (n, 1), 0) // tq
    outs = []
    for kh in range(SW_PAIR):
        head0 = (pair * SW_PAIR + kh) * SW_GROUP
        slope = jnp.zeros((n, 1), F32)
        sink = jnp.zeros((n, 1), F32)
        for g in range(SW_GROUP):
            slope = jnp.where(grp == g, slopes_ref[head0 + g], slope)
            sink = jnp.where(grp == g, sinks_ref[head0 + g], sink)
        qm = jnp.where(low if kh == 0 else jnp.logical_not(low), q, 0.0).astype(BF16)
        s = _dot_nt(qm, k) * (SW_HD ** -0.5)
        s = jnp.where(valid, s - slope * distf, -jnp.inf)
        m = jnp.maximum(jnp.max(s, -1, keepdims=True), sink)
        p = jnp.exp(s - m)
        p = p / (jnp.sum(p, -1, keepdims=True) + jnp.exp(sink - m))
        outs.append(jnp.dot(p.astype(BF16), v, preferred_element_type=F32))
    o = jnp.where(low, outs[0], outs[1])
    for g in range(SW_GROUP):
        o_ref[:, g * LANES:(g + 1) * LANES] = o[g * tq:(g + 1) * tq].astype(o_ref.dtype)


def _swa(proj, k_prev_src, v_prev_src, slopes, sinks, *, nb, nblk, tq, row0, prev_from_proj):
    qw = SW_PAIR * SW_GROUP * SW_HD
    kw = SW_PAIR * SW_HD
    qb, kb, vb = OFF_SQ // qw, OFF_SK // kw, OFF_SV // kw
    cur_row = lambda b, n, p: row0 // tq + b * nblk + n
    if prev_from_proj:
        prev_k = pl.BlockSpec((WINDOW, kw), lambda b, n, p: (b * nblk + jnp.maximum(n - 1, 0), kb + p))
        prev_v = pl.BlockSpec((WINDOW, kw), lambda b, n, p: (b * nblk + jnp.maximum(n - 1, 0), vb + p))
    else:
        prev_k = pl.BlockSpec((WINDOW, kw), lambda b, n, p: (b, p))
        prev_v = pl.BlockSpec((WINDOW, kw), lambda b, n, p: (b, p))
    smem = pl.BlockSpec(memory_space=pltpu.SMEM)
    return pl.pallas_call(
        functools.partial(_swa_kernel, tq=tq, first_prev_valid=not prev_from_proj),
        out_shape=jax.ShapeDtypeStruct((nb * nblk * tq, SW_Q_W), BF16),
        grid=(nb, nblk, SW_KV_HEADS // SW_PAIR),
        in_specs=[smem, smem,
                  pl.BlockSpec((tq, qw), lambda b, n, p: (cur_row(b, n, p), qb + p)),
                  prev_k,
                  pl.BlockSpec((tq, kw), lambda b, n, p: (cur_row(b, n, p), kb + p)),
                  prev_v,
                  pl.BlockSpec((tq, kw), lambda b, n, p: (cur_row(b, n, p), vb + p))],
        out_specs=pl.BlockSpec((tq, qw), lambda b, n, p: (b * nblk + n, p)),
        compiler_params=_cparams(("parallel", "parallel", "parallel")),
        name="swa",
    )(slopes, sinks, proj, k_prev_src, proj, v_prev_src, proj)


def _pair_major(w, axis):
    shape = w.shape
    split = shape[:axis] + (SW_KV_HEADS // SW_PAIR, SW_PAIR, SW_GROUP, SW_HD) + shape[axis + 1:]
    perm = list(range(len(split)))
    perm[axis + 1], perm[axis + 2] = perm[axis + 2], perm[axis + 1]
    return w.reshape(split).transpose(perm).reshape(shape)


def _prep_in_proj(w_in):
    sq = _pair_major(w_in[:, :, ORIG_SQ:ORIG_SQ + SW_Q_W], 2)
    main = jnp.concatenate([w_in[:, :, :ORIG_B], sq, w_in[:, :, ORIG_SQ + SW_Q_W:]], axis=2).astype(BF16)
    side = jnp.zeros(w_in.shape[:2] + (2 * LANES,), BF16)
    side = side.at[:, :, :DN_HEADS].set(w_in[:, :, ORIG_B:ORIG_A].astype(BF16))
    side = side.at[:, :, LANES:LANES + DN_HEADS].set(w_in[:, :, ORIG_A:ORIG_SQ].astype(BF16))
    return main, side


def _ffn_half_step(x, ln, wg, wu, wd, layer, name):
    h = _rmsnorm(x, ln, BF16)
    hidden = _swiglu_up(h, wg, wu, layer)
    return _residual_matmul(hidden, wd, layer, x, 0.5, DOWN_TK, TN_HALF, name)


def _rows(a, r0, r1, c0, c1):
    return lax.slice(a, (r0, c0), (r1, c1))


def kernel(x_prompt, x_sample, state_dn_conv, state_dn_recurrent, cache_swa_k, cache_swa_v, ln_ffn1, w_ffn1_gate, w_ffn1_up, w_ffn1_down, ln_mix, w_in, dn_conv_w, dn_a_log, dn_dt_bias, dn_norm_w, swa_sinks, w_branch_a, w_branch_b, w_out, ln_ffn2, w_ffn2_gate, w_ffn2_up, w_ffn2_down, ln_final):
    slopes = 2.0 ** (-8.0 * jnp.arange(1, SW_HEADS + 1, dtype=F32) / SW_HEADS)
    xs = jnp.pad(x_sample, ((0, 0), (0, DEC_PAD - DEC_SEQ), (0, 0)))
    x = jnp.concatenate([x_prompt.reshape(M_PROMPT, D_MODEL), xs.reshape(M_SAMPLE, D_MODEL)], axis=0)
    zero_conv = jnp.zeros((BATCH, SUBLANES, DN_CONV_DIM), F32)
    zero_state = jnp.zeros((BATCH, DN_HEADS, DN_DK, DN_DV), F32)

    wg1, wu1, wd1 = (w.astype(BF16) for w in (w_ffn1_gate, w_ffn1_up, w_ffn1_down))
    wg2, wu2, wd2 = (w.astype(BF16) for w in (w_ffn2_gate, w_ffn2_up, w_ffn2_down))
    w_main, w_side = _prep_in_proj(w_in)
    w_a = w_branch_a.astype(BF16)
    w_b = _pair_major(w_branch_b, 1).astype(BF16)
    w_o = w_out.astype(BF16)
    buf = cache_swa_k.shape[2]
    keep = min(WINDOW, SEQ)

    outs = [[] for _ in range(8)]
    for l in range(DEPTH):
        x = _ffn_half_step(x, ln_ffn1[l], wg1, wu1, wd1, l, "ffn1_down")

        h = _rmsnorm(x, ln_mix[l], BF16)
        proj = _matmul(h, w_main, l, F32, TN, "in_proj")
        gates = _matmul(h, w_side, l, F32, 2 * LANES, "in_proj_gates")

        conv_s = jnp.pad(state_dn_conv[l], ((0, 0), (SUBLANES - (DN_CONV - 1), 0), (0, 0)))
        dn_args = (dn_conv_w[l], dn_a_log[l], dn_dt_bias[l], dn_norm_w[l])
        oa_p, rec_p = _deltanet(proj, gates, zero_conv, zero_state, *dn_args,
                                nb=BATCH, t=SEQ, c=DN_CHUNK, t_valid=DN_CHUNK, row0=0)
        oa_s, rec_s = _deltanet(proj, gates, conv_s, state_dn_recurrent[l].astype(F32), *dn_args,
                                nb=DEC_BATCH, t=DEC_PAD, c=DEC_PAD, t_valid=DEC_SEQ, row0=M_PROMPT)

        ck = cache_swa_k[l].reshape(DEC_BATCH * buf, SW_KV_W)
        cv = cache_swa_v[l].reshape(DEC_BATCH * buf, SW_KV_W)
        sinks = swa_sinks[l].astype(F32)
        ob_p = _swa(proj, proj, proj, slopes, sinks, nb=BATCH, nblk=SEQ // WINDOW,
                    tq=WINDOW, row0=0, prev_from_proj=True)
        ob_s = _swa(proj, ck, cv, slopes, sinks, nb=DEC_BATCH, nblk=1,
                    tq=DEC_PAD, row0=M_PROMPT, prev_from_proj=False)

        o_a = jnp.concatenate([oa_p, oa_s], axis=0)
        o_b = jnp.concatenate([ob_p, ob_s], axis=0)
        merged = _branch_merge(o_a, o_b, w_a, w_b, l, proj)
        x = _residual_matmul(merged, w_o, l, x, 1.0, OUT_TK, TN, "out_proj")

        x = _ffn_half_step(x, ln_ffn2[l], wg2, wu2, wd2, l, "ffn2_down")

        last = [(b + 1) * SEQ for b in range(BATCH)]
        ps = _rows(proj, M_PROMPT, M_ROWS, 0, OFF_GA).reshape(DEC_BATCH, DEC_PAD, OFF_GA)[:, :DEC_SEQ]
        outs[0].append(jnp.stack([_rows(proj, e - (DN_CONV - 1), e, OFF_QKV, OFF_QKV + DN_CONV_DIM) for e in last]))
        outs[1].append(rec_p)
        outs[2].append(jnp.stack([_rows(proj, e - keep, e, OFF_SK, OFF_SK + SW_KV_W) for e in last])
                       .reshape(BATCH, keep, SW_KV_HEADS, SW_HD))
        outs[3].append(jnp.stack([_rows(proj, e - keep, e, OFF_SV, OFF_SV + SW_KV_W) for e in last])
                       .reshape(BATCH, keep, SW_KV_HEADS, SW_HD))
        outs[4].append(ps[:, DEC_SEQ - (DN_CONV - 1):, OFF_QKV:OFF_QKV + DN_CONV_DIM])
        outs[5].append(rec_s)
        new_k = ps[:, :, OFF_SK:OFF_SK + SW_KV_W].reshape(DEC_BATCH, DEC_SEQ, SW_KV_HEADS, SW_HD)
        new_v = ps[:, :, OFF_SV:OFF_SV + SW_KV_W].reshape(DEC_BATCH, DEC_SEQ, SW_KV_HEADS, SW_HD)
        outs[6].append(jnp.concatenate([cache_swa_k[l], new_k], axis=1)[:, -buf:])
        outs[7].append(jnp.concatenate([cache_swa_v[l], new_v], axis=1)[:, -buf:])

    y_prompt = _rmsnorm(x, ln_final, F32, 0, M_PROMPT).reshape(BATCH, SEQ, D_MODEL)
    y_sample = _rmsnorm(x, ln_final, F32, M_PROMPT, M_SAMPLE).reshape(DEC_BATCH, DEC_PAD, D_MODEL)[:, :DEC_SEQ]
    rec_dtype = state_dn_recurrent.dtype
    return (y_prompt, y_sample,
            jnp.stack(outs[0]), jnp.stack(outs[1]).astype(rec_dtype), jnp.stack(outs[2]), jnp.stack(outs[3]),
            jnp.stack(outs[4]), jnp.stack(outs[5]).astype(rec_dtype), jnp.stack(outs[6]), jnp.stack(outs[7]))
```

```python
import functools
import math

import jax
import jax.numpy as jnp
from jax import lax
from jax.experimental import pallas as pl
from jax.experimental.pallas import tpu as pltpu

F32 = jnp.float32
BF16 = jnp.bfloat16
HIGHEST = lax.Precision.HIGHEST

D_MODEL = 4096
BATCH = 4
SEQ = 2048
DEPTH = 2
DEC_BATCH = 32
DEC_SEQ = 4
DN_HEADS = 16
DN_DK = 128
DN_DV = 128
DN_CONV = 4
DN_CHUNK = 64
DN_QK_W = DN_HEADS * DN_DK
DN_V_W = DN_HEADS * DN_DV
DN_CONV_DIM = 2 * DN_QK_W + DN_V_W
SW_HEADS = 32
SW_KV_HEADS = 8
SW_GROUP = SW_HEADS // SW_KV_HEADS
SW_HD = 64
SW_Q_W = SW_HEADS * SW_HD
SW_KV_W = SW_KV_HEADS * SW_HD
WINDOW = 128
D_FF = 11008
EPS = 1e-6

SUBLANES = 8
LANES = 128
VMEM_LIMIT = 56 * 1024 * 1024

DEC_PAD = SUBLANES
M_PROMPT = BATCH * SEQ
M_SAMPLE = DEC_BATCH * DEC_PAD
M_ROWS = M_PROMPT + M_SAMPLE
TM = M_ROWS // 8
TR = 256
TN = 1024
TN_HALF = 512

OFF_QKV = 0
OFF_Z = OFF_QKV + DN_CONV_DIM
DN_PART_W = OFF_Z + DN_V_W
OFF_SQ = 0
OFF_SK = OFF_SQ + SW_Q_W
OFF_SV = OFF_SK + SW_KV_W
OFF_GA = OFF_SV + SW_KV_W
OFF_GB = OFF_GA + D_MODEL
SW_PART_W = OFF_GB + D_MODEL
ORIG_B = DN_PART_W
ORIG_A = ORIG_B + DN_HEADS
ORIG_SQ = ORIG_A + DN_HEADS

DOWN_TK = D_FF // 2
OUT_TK = D_MODEL // 2

DN_STACK = 4
SW_PAIR = 2


def _cparams(sem):
    return pltpu.CompilerParams(dimension_semantics=sem, vmem_limit_bytes=VMEM_LIMIT)


def _rmsnorm_kernel(x_ref, w_ref, o_ref):
    x = x_ref[...]
    y = x * lax.rsqrt(jnp.mean(x * x, -1, keepdims=True) + EPS)
    o_ref[...] = (y * w_ref[...]).astype(o_ref.dtype)


def _rmsnorm(x, w, out_dtype, row0=0, nrows=None):
    d = x.shape[1]
    nrows = x.shape[0] if nrows is None else nrows
    blk0 = row0 // TR
    return pl.pallas_call(
        _rmsnorm_kernel,
        out_shape=jax.ShapeDtypeStruct((nrows, d), out_dtype),
        grid=(nrows // TR,),
        in_specs=[pl.BlockSpec((TR, d), lambda i: (blk0 + i, 0)), pl.BlockSpec((1, d), lambda i: (0, 0))],
        out_specs=pl.BlockSpec((TR, d), lambda i: (i, 0)),
        compiler_params=_cparams(("parallel",)),
        name="rmsnorm",
    )(x, w.reshape(1, d))


def _mm_kernel(a_ref, w_ref, o_ref):
    o_ref[...] = jnp.dot(a_ref[...], w_ref[...], preferred_element_type=F32).astype(o_ref.dtype)


def _matmul(a, w, layer, out_dtype, tn, name):
    m, k = a.shape
    n = w.shape[2]
    return pl.pallas_call(
        _mm_kernel,
        out_shape=jax.ShapeDtypeStruct((m, n), out_dtype),
        grid=(m // TM, n // tn),
        in_specs=[pl.BlockSpec((TM, k), lambda i, j: (i, 0)),
                  pl.BlockSpec((None, k, tn), lambda i, j: (layer, 0, j))],
        out_specs=pl.BlockSpec((TM, tn), lambda i, j: (i, j)),
        compiler_params=_cparams(("parallel", "parallel")),
        name=name,
    )(a, w)


def _swiglu_kernel(a_ref, wg_ref, wu_ref, o_ref):
    a = a_ref[...]
    g = jnp.dot(a, wg_ref[...], preferred_element_type=F32)
    u = jnp.dot(a, wu_ref[...], preferred_element_type=F32)
    o_ref[...] = (g * jax.nn.sigmoid(g) * u).astype(o_ref.dtype)


def _swiglu_up(a, wg, wu, layer):
    m, k = a.shape
    n = wg.shape[2]
    tn = TN_HALF
    wspec = pl.BlockSpec((None, k, tn), lambda i, j: (layer, 0, j))
    return pl.pallas_call(
        _swiglu_kernel,
        out_shape=jax.ShapeDtypeStruct((m, n), BF16),
        grid=(m // TM, pl.cdiv(n, tn)),
        in_specs=[pl.BlockSpec((TM, k), lambda i, j: (i, 0)), wspec, wspec],
        out_specs=pl.BlockSpec((TM, tn), lambda i, j: (i, j)),
        compiler_params=_cparams(("parallel", "parallel")),
        name="ffn_gate_up",
    )(a, wg, wu)


def _residual_mm_kernel(a_ref, w_ref, r_ref, o_ref, *, scale):
    d = jnp.dot(a_ref[...], w_ref[...], preferred_element_type=F32)
    if scale != 1.0:
        d = d * scale

    @pl.when(pl.program_id(2) == 0)
    def _():
        o_ref[...] = r_ref[...] + d

    @pl.when(pl.program_id(2) != 0)
    def _():
        o_ref[...] += d


def _residual_matmul(a, w, layer, res, scale, tk, tn, name):
    m, k = a.shape
    n = w.shape[2]
    return pl.pallas_call(
        functools.partial(_residual_mm_kernel, scale=scale),
        out_shape=jax.ShapeDtypeStruct((m, n), F32),
        grid=(m // TM, n // tn, k // tk),
        in_specs=[pl.BlockSpec((TM, tk), lambda i, j, kk: (i, kk)),
                  pl.BlockSpec((None, tk, tn), lambda i, j, kk: (layer, kk, j)),
                  pl.BlockSpec((TM, tn), lambda i, j, kk: (i, j))],
        out_specs=pl.BlockSpec((TM, tn), lambda i, j, kk: (i, j)),
        compiler_params=_cparams(("parallel", "parallel", "arbitrary")),
        name=name,
    )(a, w, res)


def _merge_kernel(oa_ref, ob_ref, wa_ref, wb_ref, ga_ref, gb_ref, o_ref):
    ya = jnp.dot(oa_ref[...], wa_ref[...], preferred_element_type=F32)
    yb = jnp.dot(ob_ref[...], wb_ref[...], preferred_element_type=F32)
    o = jax.nn.sigmoid(ga_ref[...]) * ya + jax.nn.sigmoid(gb_ref[...]) * yb
    o_ref[...] = o.astype(o_ref.dtype)


def _branch_merge(o_a, o_b, w_a, w_b, layer, proj):
    m, ka = o_a.shape
    kb = o_b.shape[1]
    n = w_a.shape[2]
    tn = TN_HALF
    ga_blk = OFF_GA // tn
    gb_blk = OFF_GB // tn
    return pl.pallas_call(
        _merge_kernel,
        out_shape=jax.ShapeDtypeStruct((m, n), BF16),
        grid=(m // TM, n // tn),
        in_specs=[pl.BlockSpec((TM, ka), lambda i, j: (i, 0)),
                  pl.BlockSpec((TM, kb), lambda i, j: (i, 0)),
                  pl.BlockSpec((None, ka, tn), lambda i, j: (layer, 0, j)),
                  pl.BlockSpec((None, kb, tn), lambda i, j: (layer, 0, j)),
                  pl.BlockSpec((TM, tn), lambda i, j: (i, ga_blk + j)),
                  pl.BlockSpec((TM, tn), lambda i, j: (i, gb_blk + j))],
        out_specs=pl.BlockSpec((TM, tn), lambda i, j: (i, j)),
        compiler_params=_cparams(("parallel", "parallel")),
        name="branch_merge",
    )(o_a, o_b, w_a, w_b, proj, proj)


def _dot_nt(a, b, precision=None):
    return lax.dot_general(a, b, (((1,), (1,)), ((), ())), precision=precision,
                           preferred_element_type=F32)


def _dot_tn(a, b):
    return lax.dot_general(a, b, (((0,), (0,)), ((), ())), preferred_element_type=F32)


def _dot_hi(a, b):
    return jnp.dot(a, b, precision=HIGHEST, preferred_element_type=F32)


def _split(x):
    hi = x.astype(BF16)
    return hi, (x - hi.astype(F32)).astype(BF16)


def _dot_3x(a, b):
    return (jnp.dot(a[0], b[0], preferred_element_type=F32)
            + jnp.dot(a[0], b[1], preferred_element_type=F32)
            + jnp.dot(a[1], b[0], preferred_element_type=F32))


def _dot_b(a, b):
    return jnp.dot(a.astype(BF16), b.astype(BF16), preferred_element_type=F32)


def _unit_lower_inverse_minus_eye(a, n, c):
    ri = lax.broadcasted_iota(jnp.int32, (n, n), 0)
    ci = lax.broadcasted_iota(jnp.int32, (n, n), 1)
    base = min(16, c)
    p = jnp.where((ri // base) == (ci // base), a, 0.0)
    nt = -p
    for _ in range(int(math.log2(base)) - 1):
        p = _dot_b(p, p)
        nt = nt + p + _dot_b(nt, p)
    blk = base
    while blk < c:
        pair = (ri // (2 * blk)) == (ci // (2 * blk))
        lower = pair & (((ri // blk) % 2) == 1) & (((ci // blk) % 2) == 0)
        l = jnp.where(lower, a, 0.0)
        lt = l + _dot_b(l, nt)
        nt = nt - lt - _dot_b(nt, lt)
        blk *= 2
    return nt


def _dn_kernel(qc_ref, kc_ref, vc_ref, qp_ref, kp_ref, vp_ref, qi_ref, ki_ref, vi_ref,
               z_ref, b_ref, a_ref, wq_ref, wk_ref, wv_ref, alog_ref, dtb_ref, nw_ref, s0_ref,
               o_ref, sout_ref, s_ref, ext_ref, *, c, t_valid):
    ch = pl.program_id(1)
    nch = pl.num_programs(1)
    n = DN_STACK * c

    @pl.when(ch == 0)
    def _():
        s_ref[...] = s0_ref[0]

    def conv(cur_ref, prev_ref, init_ref, w_ref):
        prev = jnp.where(ch == 0, init_ref[0], prev_ref[...])
        ext_ref[0:SUBLANES, :] = prev
        ext_ref[SUBLANES:SUBLANES + c, :] = cur_ref[...]
        first = SUBLANES - (DN_CONV - 1)
        y = ext_ref[first:first + c, :] * w_ref[0:1, :]
        for j in range(1, DN_CONV):
            y = y + ext_ref[first + j:first + j + c, :] * w_ref[j:j + 1, :]
        return y * jax.nn.sigmoid(y)

    yq = conv(qc_ref, qp_ref, qi_ref, wq_ref)
    yk = conv(kc_ref, kp_ref, ki_ref, wk_ref)
    yv = conv(vc_ref, vp_ref, vi_ref, wv_ref)
    zz = z_ref[...]

    beta_all = jax.nn.sigmoid(b_ref[...])
    araw = a_ref[...] + dtb_ref[...]
    softplus = jnp.maximum(araw, 0.0) + jnp.log1p(jnp.exp(-jnp.abs(araw)))
    g_all = -jnp.exp(alog_ref[...]) * softplus
    if t_valid < c:
        row_ok = lax.broadcasted_iota(jnp.int32, (c, LANES), 0) < t_valid
        beta_all = jnp.where(row_ok, beta_all, 0.0)
        g_all = jnp.where(row_ok, g_all, 0.0)
    tri = lax.broadcasted_iota(jnp.int32, (c, c), 0) >= lax.broadcasted_iota(jnp.int32, (c, c), 1)
    g_cum = _dot_hi(tri.astype(F32), g_all)
    lri = lax.broadcasted_iota(jnp.int32, (LANES, LANES), 0)
    lci = lax.broadcasted_iota(jnp.int32, (LANES, LANES), 1)
    g_cum_t = _dot_nt((lri == lci).astype(F32), g_cum, HIGHEST)

    ri = lax.broadcasted_iota(jnp.int32, (n, n), 0)
    ci = lax.broadcasted_iota(jnp.int32, (n, n), 1)
    same_head = (ri // c) == (ci // c)
    causal = same_head & ((ri % c) >= (ci % c))
    strict = same_head & ((ri % c) > (ci % c))

    for grp in range(DN_HEADS // DN_STACK):
        heads = [grp * DN_STACK + j for j in range(DN_STACK)]

        def stack(y):
            return jnp.concatenate([y[:, h * DN_DK:(h + 1) * DN_DK] for h in heads], axis=0)

        def column(m):
            return jnp.concatenate([m[:, h:h + 1] for h in heads], axis=0)

        q = stack(yq)
        k = stack(yk)
        v = stack(yv)
        bcol = column(beta_all)
        gcol = column(g_cum)
        grow = jnp.concatenate([g_cum_t[h:h + 1, :] for h in heads], axis=1)
        glast = jnp.concatenate([jnp.broadcast_to(g_cum[c - 1:c, h:h + 1], (c, 1)) for h in heads], axis=0)
        q = q * lax.rsqrt(jnp.sum(q * q, -1, keepdims=True) + 1e-6) * (DN_DK ** -0.5)
        k = k * lax.rsqrt(jnp.sum(k * k, -1, keepdims=True) + 1e-6)
        decay = jnp.exp(jnp.where(causal, gcol - grow, -jnp.inf))
        kb = k * bcol
        k_b = k.astype(BF16)
        a_mat = jnp.where(strict, _dot_nt(kb.astype(BF16), k_b) * decay, 0.0)
        t_low = _unit_lower_inverse_minus_eye(a_mat, n, c)
        eg = jnp.exp(gcol)
        rhs = jnp.concatenate([v * bcol, kb * eg], axis=1)
        sol = rhs + _dot_3x(_split(t_low), _split(rhs))
        u_base = sol[:, :DN_DV]
        w_cum = sol[:, DN_DV:].astype(BF16)
        a_qk = (_dot_nt(q.astype(BF16), k_b) * decay).astype(BF16)
        q_dec = (q * eg).astype(BF16)
        k_dec = (k * jnp.exp(glast - gcol)).astype(BF16)

        s_old = [s_ref[h] for h in heads]
        ws, qs = [], []
        for j in range(DN_STACK):
            rows = slice(j * c, (j + 1) * c)
            lhs = jnp.concatenate([w_cum[rows], q_dec[rows]], axis=0)
            r = jnp.dot(lhs, s_old[j].astype(BF16), preferred_element_type=F32)
            ws.append(r[:c])
            qs.append(r[c:])
        u = u_base - jnp.concatenate(ws, axis=0)
        u_b = u.astype(BF16)
        o = jnp.concatenate(qs, axis=0) + jnp.dot(a_qk, u_b, preferred_element_type=F32)
        for j, h in enumerate(heads):
            rows = slice(j * c, (j + 1) * c)
            g_tot = jnp.exp(g_cum[c - 1:c, h:h + 1])
            s_ref[h] = s_old[j] * g_tot + _dot_tn(k_dec[rows], u_b[rows])
        o = o * lax.rsqrt(jnp.mean(o * o, -1, keepdims=True) + EPS) * nw_ref[...]
        z = stack(zz)
        o = o * (z * jax.nn.sigmoid(z))
        for j, h in enumerate(heads):
            o_ref[:, h * DN_DV:(h + 1) * DN_DV] = o[j * c:(j + 1) * c].astype(o_ref.dtype)

    @pl.when(ch == nch - 1)
    def _():
        sout_ref[0] = s_ref[...]


def _deltanet(proj, gates, conv_init, s0, conv_w, a_log, dt_bias, norm_w, *, nb, t, c, t_valid, row0):
    w = DN_QK_W
    nch = t // c
    cur_row = lambda b, k: row0 // c + b * nch + k
    prev_row = lambda b, k: jnp.maximum(row0 // SUBLANES + (b * t + k * c) // SUBLANES - 1, 0)
    kq, kk, kv, kz = (OFF_QKV // w, (OFF_QKV + DN_QK_W) // w, (OFF_QKV + 2 * DN_QK_W) // w, OFF_Z // w)

    def cur(col):
        return pl.BlockSpec((c, w), lambda b, k: (cur_row(b, k), col))

    def prev(col):
        return pl.BlockSpec((SUBLANES, w), lambda b, k: (prev_row(b, k), col))

    def init(col):
        return pl.BlockSpec((1, SUBLANES, w), lambda b, k: (b, 0, col))

    def taps(col):
        return pl.BlockSpec((DN_CONV, w), lambda b, k: (0, col))

    row = lambda: pl.BlockSpec((1, LANES), lambda b, k: (0, 0))
    pad = lambda x: jnp.zeros((1, LANES), F32).at[0, :DN_HEADS].set(x.astype(F32))
    state = pl.BlockSpec((1, DN_HEADS, DN_DK, DN_DV), lambda b, k: (b, 0, 0, 0))
    o, s_new = pl.pallas_call(
        functools.partial(_dn_kernel, c=c, t_valid=t_valid),
        out_shape=(jax.ShapeDtypeStruct((nb * t, DN_V_W), BF16),
                   jax.ShapeDtypeStruct((nb, DN_HEADS, DN_DK, DN_DV), F32)),
        grid=(nb, nch),
        in_specs=[cur(kq), cur(kk), cur(kv), prev(kq), prev(kk), prev(kv),
                  init(kq), init(kk), init(kv), cur(kz),
                  pl.BlockSpec((c, LANES), lambda b, k: (cur_row(b, k), 0)),
                  pl.BlockSpec((c, LANES), lambda b, k: (cur_row(b, k), 1)),
                  taps(kq), taps(kk), taps(kv), row(), row(), row(), state],
        out_specs=(pl.BlockSpec((c, w), lambda b, k: (b * nch + k, 0)), state),
        scratch_shapes=[pltpu.VMEM((DN_HEADS, DN_DK, DN_DV), F32),
                        pltpu.VMEM((c + SUBLANES, w), F32)],
        compiler_params=_cparams(("parallel", "arbitrary")),
        name="deltanet",
    )(proj, proj, proj, proj, proj, proj, conv_init, conv_init, conv_init, proj, gates, gates,
      conv_w, conv_w, conv_w, pad(a_log), pad(dt_bias), norm_w.reshape(1, DN_DV).astype(F32), s0)
    return o, s_new


def _swa_kernel(slopes_ref, sinks_ref, q_ref, kp_ref, kc_ref, vp_ref, vc_ref, o_ref, *, tq, first_prev_valid):
    blk = pl.program_id(1)
    pair = pl.program_id(2)
    tk = WINDOW + tq
    n = SW_GROUP * tq
    k = jnp.concatenate([kp_ref[...], kc_ref[...]], axis=0)
    v = jnp.concatenate([vp_ref[...], vc_ref[...]], axis=0)
    k_swap = pltpu.roll(k, SW_HD, 1)
    v_swap = pltpu.roll(v, SW_HD, 1)
    low_kv = lax.broadcasted_iota(jnp.int32, (tk, LANES), 1) < SW_HD
    q = q_ref[...]
    lane_low = lax.broadcasted_iota(jnp.int32, (n, LANES), 1) < SW_HD
    grp = lax.broadcasted_iota(jnp.int32, (n, 1), 0) // tq
    own_half = lane_low == ((grp % 2) == 0)
    low_out = lax.broadcasted_iota(jnp.int32, (tq, LANES), 1) < SW_HD
    ri = lax.broadcasted_iota(jnp.int32, (n, tk), 0)
    ci = lax.broadcasted_iota(jnp.int32, (n, tk), 1)
    dist = (ri % tq) + WINDOW - ci
    valid = (dist >= 0) & (dist <= WINDOW)
    if not first_prev_valid:
        valid = valid & ((ci >= WINDOW) | (blk > 0))
    distf = dist.astype(F32)
    for kh in range(SW_PAIR):
        head0 = (pair * SW_PAIR + kh) * SW_GROUP
        slope = jnp.zeros((n, 1), F32)
        sink = jnp.zeros((n, 1), F32)
        for g in range(SW_GROUP):
            slope = jnp.where(grp == g, slopes_ref[head0 + g], slope)
            sink = jnp.where(grp == g, sinks_ref[head0 + g], sink)
        kk = (jnp.where(low_kv, k, k_swap) if kh == 0 else jnp.where(low_kv, k_swap, k)).astype(BF16)
        vv = (jnp.where(low_kv, v, v_swap) if kh == 0 else jnp.where(low_kv, v_swap, v)).astype(BF16)
        blocks = [kh * (SW_GROUP // 2) + g // 2 for g in range(SW_GROUP)]
        qs = jnp.concatenate([q[:, b * LANES:(b + 1) * LANES] for b in blocks], axis=0)
        qm = jnp.where(own_half, qs, 0.0).astype(BF16)
        s = _dot_nt(qm, kk) * (SW_HD ** -0.5)
        s = jnp.where(valid, s - slope * distf, -jnp.inf)
        m = jnp.maximum(jnp.max(s, -1, keepdims=True), sink)
        p = jnp.exp(s - m)
        p = p / (jnp.sum(p, -1, keepdims=True) + jnp.exp(sink - m))
        o = jnp.dot(p.astype(BF16), vv, preferred_element_type=F32)
        for j in range(SW_GROUP // 2):
            even = o[(2 * j) * tq:(2 * j + 1) * tq]
            odd = o[(2 * j + 1) * tq:(2 * j + 2) * tq]
            b = kh * (SW_GROUP // 2) + j
            o_ref[:, b * LANES:(b + 1) * LANES] = jnp.where(low_out, even, odd).astype(o_ref.dtype)


def _swa(proj, k_prev_src, v_prev_src, slopes, sinks, *, nb, nblk, tq, row0, prev_from_proj):
    qw = SW_PAIR * SW_GROUP * SW_HD
    kw = SW_PAIR * SW_HD
    qb, kb, vb = OFF_SQ // qw, OFF_SK // kw, OFF_SV // kw
    cur_row = lambda b, n, p: row0 // tq + b * nblk + n
    if prev_from_proj:
        prev_k = pl.BlockSpec((WINDOW, kw), lambda b, n, p: (b * nblk + jnp.maximum(n - 1, 0), kb + p))
        prev_v = pl.BlockSpec((WINDOW, kw), lambda b, n, p: (b * nblk + jnp.maximum(n - 1, 0), vb + p))
    else:
        prev_k = pl.BlockSpec((WINDOW, kw), lambda b, n, p: (b, p))
        prev_v = pl.BlockSpec((WINDOW, kw), lambda b, n, p: (b, p))
    smem = pl.BlockSpec(memory_space=pltpu.SMEM)
    return pl.pallas_call(
        functools.partial(_swa_kernel, tq=tq, first_prev_valid=not prev_from_proj),
        out_shape=jax.ShapeDtypeStruct((nb * nblk * tq, SW_Q_W), BF16),
        grid=(nb, nblk, SW_KV_HEADS // SW_PAIR),
        in_specs=[smem, smem,
                  pl.BlockSpec((tq, qw), lambda b, n, p: (cur_row(b, n, p), qb + p)),
                  prev_k,
                  pl.BlockSpec((tq, kw), lambda b, n, p: (cur_row(b, n, p), kb + p)),
                  prev_v,
                  pl.BlockSpec((tq, kw), lambda b, n, p: (cur_row(b, n, p), vb + p))],
        out_specs=pl.BlockSpec((tq, qw), lambda b, n, p: (b * nblk + n, p)),
        compiler_params=_cparams(("parallel", "parallel", "parallel")),
        name="swa",
    )(slopes, sinks, proj, k_prev_src, proj, v_prev_src, proj)


def _prep_in_proj(w_in):
    dn_part = w_in[:, :, :ORIG_B].astype(BF16)
    sw_part = w_in[:, :, ORIG_SQ:].astype(BF16)
    side = jnp.zeros(w_in.shape[:2] + (2 * LANES,), BF16)
    side = side.at[:, :, :DN_HEADS].set(w_in[:, :, ORIG_B:ORIG_A].astype(BF16))
    side = side.at[:, :, LANES:LANES + DN_HEADS].set(w_in[:, :, ORIG_A:ORIG_SQ].astype(BF16))
    return dn_part, sw_part, side


def _ffn_half_step(x, ln, wg, wu, wd, layer, name):
    h = _rmsnorm(x, ln, BF16)
    hidden = _swiglu_up(h, wg, wu, layer)
    return _residual_matmul(hidden, wd, layer, x, 0.5, DOWN_TK, TN_HALF, name)


def _rows(a, r0, r1, c0, c1):
    return lax.slice(a, (r0, c0), (r1, c1))


def kernel(x_prompt, x_sample, state_dn_conv, state_dn_recurrent, cache_swa_k, cache_swa_v, ln_ffn1, w_ffn1_gate, w_ffn1_up, w_ffn1_down, ln_mix, w_in, dn_conv_w, dn_a_log, dn_dt_bias, dn_norm_w, swa_sinks, w_branch_a, w_branch_b, w_out, ln_ffn2, w_ffn2_gate, w_ffn2_up, w_ffn2_down, ln_final):
    slopes = 2.0 ** (-8.0 * jnp.arange(1, SW_HEADS + 1, dtype=F32) / SW_HEADS)
    xs = jnp.pad(x_sample, ((0, 0), (0, DEC_PAD - DEC_SEQ), (0, 0)))
    x = jnp.concatenate([x_prompt.reshape(M_PROMPT, D_MODEL), xs.reshape(M_SAMPLE, D_MODEL)], axis=0)
    zero_conv = jnp.zeros((BATCH, SUBLANES, DN_CONV_DIM), F32)
    zero_state = jnp.zeros((BATCH, DN_HEADS, DN_DK, DN_DV), F32)

    wg1, wu1, wd1 = (w.astype(BF16) for w in (w_ffn1_gate, w_ffn1_up, w_ffn1_down))
    wg2, wu2, wd2 = (w.astype(BF16) for w in (w_ffn2_gate, w_ffn2_up, w_ffn2_down))
    w_dn, w_sw, w_side = _prep_in_proj(w_in)
    w_a = w_branch_a.astype(BF16)
    w_b = w_branch_b.astype(BF16)
    w_o = w_out.astype(BF16)
    buf = cache_swa_k.shape[2]
    keep = min(WINDOW, SEQ)

    outs = [[] for _ in range(8)]
    for l in range(DEPTH):
        x = _ffn_half_step(x, ln_ffn1[l], wg1, wu1, wd1, l, "ffn1_down")

        h = _rmsnorm(x, ln_mix[l], BF16)
        proj_dn = _matmul(h, w_dn, l, F32, TN, "in_proj_dn")
        proj_sw = _matmul(h, w_sw, l, F32, TN, "in_proj_sw")
        gates = _matmul(h, w_side, l, F32, 2 * LANES, "in_proj_gates")

        conv_s = jnp.pad(state_dn_conv[l], ((0, 0), (SUBLANES - (DN_CONV - 1), 0), (0, 0)))
        dn_args = (dn_conv_w[l], dn_a_log[l], dn_dt_bias[l], dn_norm_w[l])
        oa_p, rec_p = _deltanet(proj_dn, gates, zero_conv, zero_state, *dn_args,
                                nb=BATCH, t=SEQ, c=DN_CHUNK, t_valid=DN_CHUNK, row0=0)
        oa_s, rec_s = _deltanet(proj_dn, gates, conv_s, state_dn_recurrent[l].astype(F32), *dn_args,
                                nb=DEC_BATCH, t=DEC_PAD, c=DEC_PAD, t_valid=DEC_SEQ, row0=M_PROMPT)

        ck = cache_swa_k[l].reshape(DEC_BATCH * buf, SW_KV_W)
        cv = cache_swa_v[l].reshape(DEC_BATCH * buf, SW_KV_W)
        sinks = swa_sinks[l].astype(F32)
        ob_p = _swa(proj_sw, proj_sw, proj_sw, slopes, sinks, nb=BATCH, nblk=SEQ // WINDOW,
                    tq=WINDOW, row0=0, prev_from_proj=True)
        ob_s = _swa(proj_sw, ck, cv, slopes, sinks, nb=DEC_BATCH, nblk=1,
                    tq=DEC_PAD, row0=M_PROMPT, prev_from_proj=False)

        o_a = jnp.concatenate([oa_p, oa_s], axis=0)
        o_b = jnp.concatenate([ob_p, ob_s], axis=0)
        merged = _branch_merge(o_a, o_b, w_a, w_b, l, proj_sw)
        x = _residual_matmul(merged, w_o, l, x, 1.0, OUT_TK, TN, "out_proj")

        x = _ffn_half_step(x, ln_ffn2[l], wg2, wu2, wd2, l, "ffn2_down")

        last = [(b + 1) * SEQ for b in range(BATCH)]
        sample = lambda p, c0, c1: _rows(p, M_PROMPT, M_ROWS, c0, c1).reshape(DEC_BATCH, DEC_PAD, c1 - c0)[:, :DEC_SEQ]
        outs[0].append(jnp.stack([_rows(proj_dn, e - (DN_CONV - 1), e, OFF_QKV, OFF_QKV + DN_CONV_DIM) for e in last]))
        outs[1].append(rec_p)
        outs[2].append(jnp.stack([_rows(proj_sw, e - keep, e, OFF_SK, OFF_SK + SW_KV_W) for e in last])
                       .reshape(BATCH, keep, SW_KV_HEADS, SW_HD))
        outs[3].append(jnp.stack([_rows(proj_sw, e - keep, e, OFF_SV, OFF_SV + SW_KV_W) for e in last])
                       .reshape(BATCH, keep, SW_KV_HEADS, SW_HD))
        outs[4].append(sample(proj_dn, OFF_QKV, OFF_QKV + DN_CONV_DIM)[:, DEC_SEQ - (DN_CONV - 1):])
        outs[5].append(rec_s)
        new_k = sample(proj_sw, OFF_SK, OFF_SK + SW_KV_W).reshape(DEC_BATCH, DEC_SEQ, SW_KV_HEADS, SW_HD)
        new_v = sample(proj_sw, OFF_SV, OFF_SV + SW_KV_W).reshape(DEC_BATCH, DEC_SEQ, SW_KV_HEADS, SW_HD)
        outs[6].append(jnp.concatenate([cache_swa_k[l], new_k], axis=1)[:, -buf:])
        outs[7].append(jnp.concatenate([cache_swa_v[l], new_v], axis=1)[:, -buf:])

    y_prompt = _rmsnorm(x, ln_final, F32, 0, M_PROMPT).reshape(BATCH, SEQ, D_MODEL)
    y_sample = _rmsnorm(x, ln_final, F32, M_PROMPT, M_SAMPLE).reshape(DEC_BATCH, DEC_PAD, D_MODEL)[:, :DEC_SEQ]
    rec_dtype = state_dn_recurrent.dtype
    return (y_prompt, y_sample,
            jnp.stack(outs[0]), jnp.stack(outs[1]).astype(rec_dtype), jnp.stack(outs[2]), jnp.stack(outs[3]),
            jnp.stack(outs[4]), jnp.stack(outs[5]).astype(rec_dtype), jnp.stack(outs[6]), jnp.stack(outs[7]))
```

```python
import functools
import math

import jax
import jax.numpy as jnp
from jax import lax
from jax.experimental import pallas as pl
from jax.experimental.pallas import tpu as pltpu

F32 = jnp.float32
BF16 = jnp.bfloat16
HIGHEST = lax.Precision.HIGHEST

D_MODEL = 4096
BATCH = 4
SEQ = 2048
DEPTH = 2
DEC_BATCH = 32
DEC_SEQ = 4
DN_HEADS = 16
DN_DK = 128
DN_DV = 128
DN_CONV = 4
DN_CHUNK = 64
DN_QK_W = DN_HEADS * DN_DK
DN_V_W = DN_HEADS * DN_DV
DN_CONV_DIM = 2 * DN_QK_W + DN_V_W
SW_HEADS = 32
SW_KV_HEADS = 8
SW_GROUP = SW_HEADS // SW_KV_HEADS
SW_HD = 64
SW_Q_W = SW_HEADS * SW_HD
SW_KV_W = SW_KV_HEADS * SW_HD
WINDOW = 128
D_FF = 11008
EPS = 1e-6

SUBLANES = 8
LANES = 128
VMEM_LIMIT = 56 * 1024 * 1024

DEC_PAD = SUBLANES
M_PROMPT = BATCH * SEQ
M_SAMPLE = DEC_BATCH * DEC_PAD
M_ROWS = M_PROMPT + M_SAMPLE
TM = M_ROWS // 8
TR = 256
TN = 1024
TN_HALF = 512

OFF_QKV = 0
OFF_Z = OFF_QKV + DN_CONV_DIM
DN_PART_W = OFF_Z + DN_V_W
OFF_SQ = 0
OFF_SK = OFF_SQ + SW_Q_W
OFF_SV = OFF_SK + SW_KV_W
OFF_GA = OFF_SV + SW_KV_W
OFF_GB = OFF_GA + D_MODEL
SW_PART_W = OFF_GB + D_MODEL
ORIG_B = DN_PART_W
ORIG_A = ORIG_B + DN_HEADS
ORIG_SQ = ORIG_A + DN_HEADS

FF_TN = 256
DOWN_TK = D_FF // 2
OUT_TK = D_MODEL // 2

DN_STACK = 4
DN_GROUP_BATCH = 4
SW_PAIR = 2


def _cparams(sem):
    return pltpu.CompilerParams(dimension_semantics=sem, vmem_limit_bytes=VMEM_LIMIT)


def _rmsnorm_kernel(x_ref, w_ref, o_ref):
    x = x_ref[...]
    y = x * lax.rsqrt(jnp.mean(x * x, -1, keepdims=True) + EPS)
    o_ref[...] = (y * w_ref[...]).astype(o_ref.dtype)


def _rmsnorm(x, w, out_dtype, row0=0, nrows=None):
    d = x.shape[1]
    nrows = x.shape[0] if nrows is None else nrows
    blk0 = row0 // TR
    return pl.pallas_call(
        _rmsnorm_kernel,
        out_shape=jax.ShapeDtypeStruct((nrows, d), out_dtype),
        grid=(nrows // TR,),
        in_specs=[pl.BlockSpec((TR, d), lambda i: (blk0 + i, 0)), pl.BlockSpec((1, d), lambda i: (0, 0))],
        out_specs=pl.BlockSpec((TR, d), lambda i: (i, 0)),
        compiler_params=_cparams(("parallel",)),
        name="rmsnorm",
    )(x, w.reshape(1, d))


def _mm_kernel(a_ref, w_ref, o_ref):
    o_ref[...] = jnp.dot(a_ref[...], w_ref[...], preferred_element_type=F32).astype(o_ref.dtype)


def _matmul(a, w, layer, out_dtype, tn, name):
    m, k = a.shape
    n = w.shape[2]
    return pl.pallas_call(
        _mm_kernel,
        out_shape=jax.ShapeDtypeStruct((m, n), out_dtype),
        grid=(m // TM, n // tn),
        in_specs=[pl.BlockSpec((TM, k), lambda i, j: (i, 0)),
                  pl.BlockSpec((None, k, tn), lambda i, j: (layer, 0, j))],
        out_specs=pl.BlockSpec((TM, tn), lambda i, j: (i, j)),
        compiler_params=_cparams(("parallel", "parallel")),
        name=name,
    )(a, w)


def _mm_cast_kernel(a_ref, w_ref, o_ref, wb_ref):
    @pl.when(pl.program_id(1) == 0)
    def _():
        wb_ref[...] = w_ref[...].astype(BF16)

    o_ref[...] = jnp.dot(a_ref[...], wb_ref[...], preferred_element_type=F32).astype(o_ref.dtype)


def _matmul_cast(a, w, layer, ncols, out_dtype, tn, name):
    m, k = a.shape
    return pl.pallas_call(
        _mm_cast_kernel,
        out_shape=jax.ShapeDtypeStruct((m, ncols), out_dtype),
        grid=(ncols // tn, m // TM),
        in_specs=[pl.BlockSpec((TM, k), lambda j, i: (i, 0)),
                  pl.BlockSpec((None, k, tn), lambda j, i: (layer, 0, j))],
        out_specs=pl.BlockSpec((TM, tn), lambda j, i: (i, j)),
        scratch_shapes=[pltpu.VMEM((k, tn), BF16)],
        compiler_params=_cparams(("parallel", "arbitrary")),
        name=name,
    )(a, w)


def _swiglu_kernel(a_ref, wg_ref, wu_ref, o_ref, wgb_ref, wub_ref):
    @pl.when(pl.program_id(1) == 0)
    def _():
        wgb_ref[...] = wg_ref[...].astype(BF16)
        wub_ref[...] = wu_ref[...].astype(BF16)

    a = a_ref[...]
    g = jnp.dot(a, wgb_ref[...], preferred_element_type=F32)
    u = jnp.dot(a, wub_ref[...], preferred_element_type=F32)
    o_ref[...] = (g * jax.nn.sigmoid(g) * u).astype(o_ref.dtype)


def _swiglu_up(a, wg, wu, layer):
    m, k = a.shape
    n = wg.shape[2]
    tn = FF_TN
    wspec = pl.BlockSpec((None, k, tn), lambda j, i: (layer, 0, j))
    return pl.pallas_call(
        _swiglu_kernel,
        out_shape=jax.ShapeDtypeStruct((m, n), BF16),
        grid=(n // tn, m // TM),
        in_specs=[pl.BlockSpec((TM, k), lambda j, i: (i, 0)), wspec, wspec],
        out_specs=pl.BlockSpec((TM, tn), lambda j, i: (i, j)),
        scratch_shapes=[pltpu.VMEM((k, tn), BF16), pltpu.VMEM((k, tn), BF16)],
        compiler_params=_cparams(("parallel", "arbitrary")),
        name="ffn_gate_up",
    )(a, wg, wu)


def _residual_mm_kernel(a_ref, w_ref, r_ref, o_ref, *, scale):
    d = jnp.dot(a_ref[...], w_ref[...], preferred_element_type=F32)
    if scale != 1.0:
        d = d * scale

    @pl.when(pl.program_id(2) == 0)
    def _():
        o_ref[...] = r_ref[...] + d

    @pl.when(pl.program_id(2) != 0)
    def _():
        o_ref[...] += d


def _residual_matmul(a, w, layer, res, scale, tk, tn, name):
    m, k = a.shape
    n = w.shape[2]
    return pl.pallas_call(
        functools.partial(_residual_mm_kernel, scale=scale),
        out_shape=jax.ShapeDtypeStruct((m, n), F32),
        grid=(m // TM, n // tn, k // tk),
        in_specs=[pl.BlockSpec((TM, tk), lambda i, j, kk: (i, kk)),
                  pl.BlockSpec((None, tk, tn), lambda i, j, kk: (layer, kk, j)),
                  pl.BlockSpec((TM, tn), lambda i, j, kk: (i, j))],
        out_specs=pl.BlockSpec((TM, tn), lambda i, j, kk: (i, j)),
        compiler_params=_cparams(("parallel", "parallel", "arbitrary")),
        name=name,
    )(a, w, res)


def _merge_kernel(oa_ref, ob_ref, wa_ref, wb_ref, ga_ref, gb_ref, o_ref):
    ya = jnp.dot(oa_ref[...], wa_ref[...], preferred_element_type=F32)
    yb = jnp.dot(ob_ref[...], wb_ref[...], preferred_element_type=F32)
    o = jax.nn.sigmoid(ga_ref[...]) * ya + jax.nn.sigmoid(gb_ref[...]) * yb
    o_ref[...] = o.astype(o_ref.dtype)


def _branch_merge(o_a, o_b, w_a, w_b, layer, proj):
    m, ka = o_a.shape
    kb = o_b.shape[1]
    n = w_a.shape[2]
    tn = TN_HALF
    ga_blk = OFF_GA // tn
    gb_blk = OFF_GB // tn
    return pl.pallas_call(
        _merge_kernel,
        out_shape=jax.ShapeDtypeStruct((m, n), BF16),
        grid=(m // TM, n // tn),
        in_specs=[pl.BlockSpec((TM, ka), lambda i, j: (i, 0)),
                  pl.BlockSpec((TM, kb), lambda i, j: (i, 0)),
                  pl.BlockSpec((None, ka, tn), lambda i, j: (layer, 0, j)),
                  pl.BlockSpec((None, kb, tn), lambda i, j: (layer, 0, j)),
                  pl.BlockSpec((TM, tn), lambda i, j: (i, ga_blk + j)),
                  pl.BlockSpec((TM, tn), lambda i, j: (i, gb_blk + j))],
        out_specs=pl.BlockSpec((TM, tn), lambda i, j: (i, j)),
        compiler_params=_cparams(("parallel", "parallel")),
        name="branch_merge",
    )(o_a, o_b, w_a, w_b, proj, proj)


def _dot_nt(a, b, precision=None):
    return lax.dot_general(a, b, (((1,), (1,)), ((), ())), precision=precision,
                           preferred_element_type=F32)


def _dot_tn(a, b):
    return lax.dot_general(a, b, (((0,), (0,)), ((), ())), preferred_element_type=F32)


def _dot_hi(a, b):
    return jnp.dot(a, b, precision=HIGHEST, preferred_element_type=F32)


def _split(x):
    hi = x.astype(BF16)
    return hi, (x - hi.astype(F32)).astype(BF16)


def _bdot(a, b):
    return jnp.einsum("gij,gjk->gik", a.astype(BF16), b.astype(BF16), preferred_element_type=F32)


def _bdot_nt(a, b):
    return jnp.einsum("gik,gjk->gij", a.astype(BF16), b.astype(BF16), preferred_element_type=F32)


def _bdot_3x(a, b):
    return _bdot(a[0], b[0]) + _bdot(a[0], b[1]) + _bdot(a[1], b[0])


def _unit_lower_inverse_minus_eye(a, n, c):
    ri = lax.broadcasted_iota(jnp.int32, (n, n), 0)
    ci = lax.broadcasted_iota(jnp.int32, (n, n), 1)
    base = min(16, c)
    p = jnp.where((ri // base) == (ci // base), a, 0.0)
    nt = -p
    for _ in range(int(math.log2(base)) - 1):
        p = _bdot(p, p)
        nt = nt + p + _bdot(nt, p)
    blk = base
    while blk < c:
        pair = (ri // (2 * blk)) == (ci // (2 * blk))
        lower = pair & (((ri // blk) % 2) == 1) & (((ci // blk) % 2) == 0)
        l = jnp.where(lower, a, 0.0)
        lt = l + _bdot(l, nt)
        nt = nt - lt - _bdot(nt, lt)
        blk *= 2
    return nt


def _dn_kernel(qc_ref, kc_ref, vc_ref, qp_ref, kp_ref, vp_ref, qi_ref, ki_ref, vi_ref,
               z_ref, b_ref, a_ref, wq_ref, wk_ref, wv_ref, alog_ref, dtb_ref, nw_ref, s0_ref,
               o_ref, sout_ref, s_ref, ext_ref, *, c, t_valid):
    ch = pl.program_id(1)
    nch = pl.num_programs(1)
    n = DN_STACK * c

    @pl.when(ch == 0)
    def _():
        s_ref[...] = s0_ref[0]

    def conv(cur_ref, prev_ref, init_ref, w_ref):
        prev = jnp.where(ch == 0, init_ref[0], prev_ref[...])
        ext_ref[0:SUBLANES, :] = prev
        ext_ref[SUBLANES:SUBLANES + c, :] = cur_ref[...]
        first = SUBLANES - (DN_CONV - 1)
        y = ext_ref[first:first + c, :] * w_ref[0:1, :]
        for j in range(1, DN_CONV):
            y = y + ext_ref[first + j:first + j + c, :] * w_ref[j:j + 1, :]
        return y * jax.nn.sigmoid(y)

    yq = conv(qc_ref, qp_ref, qi_ref, wq_ref)
    yk = conv(kc_ref, kp_ref, ki_ref, wk_ref)
    yv = conv(vc_ref, vp_ref, vi_ref, wv_ref)
    zz = z_ref[...]

    beta_all = jax.nn.sigmoid(b_ref[...])
    araw = a_ref[...] + dtb_ref[...]
    softplus = jnp.maximum(araw, 0.0) + jnp.log1p(jnp.exp(-jnp.abs(araw)))
    g_all = -jnp.exp(alog_ref[...]) * softplus
    if t_valid < c:
        row_ok = lax.broadcasted_iota(jnp.int32, (c, LANES), 0) < t_valid
        beta_all = jnp.where(row_ok, beta_all, 0.0)
        g_all = jnp.where(row_ok, g_all, 0.0)
    tri = lax.broadcasted_iota(jnp.int32, (c, c), 0) >= lax.broadcasted_iota(jnp.int32, (c, c), 1)
    g_cum = _dot_hi(tri.astype(F32), g_all)
    lri = lax.broadcasted_iota(jnp.int32, (LANES, LANES), 0)
    lci = lax.broadcasted_iota(jnp.int32, (LANES, LANES), 1)
    g_cum_t = _dot_nt((lri == lci).astype(F32), g_cum, HIGHEST)

    ri = lax.broadcasted_iota(jnp.int32, (n, n), 0)
    ci = lax.broadcasted_iota(jnp.int32, (n, n), 1)
    same_head = (ri // c) == (ci // c)
    causal = same_head & ((ri % c) >= (ci % c))
    strict = same_head & ((ri % c) > (ci % c))

    for g0 in range(0, DN_HEADS // DN_STACK, DN_GROUP_BATCH):
        groups = [[(g0 + gi) * DN_STACK + j for j in range(DN_STACK)] for gi in range(DN_GROUP_BATCH)]

        def stack(y):
            return jnp.stack([jnp.concatenate([y[:, h * DN_DK:(h + 1) * DN_DK] for h in hs], axis=0) for hs in groups])

        def column(m):
            return jnp.stack([jnp.concatenate([m[:, h:h + 1] for h in hs], axis=0) for hs in groups])

        q = stack(yq)
        k = stack(yk)
        v = stack(yv)
        bcol = column(beta_all)
        gcol = column(g_cum)
        grow = jnp.stack([jnp.concatenate([g_cum_t[h:h + 1, :] for h in hs], axis=1) for hs in groups])
        glast = jnp.stack([jnp.concatenate([jnp.broadcast_to(g_cum[c - 1:c, h:h + 1], (c, 1)) for h in hs], axis=0)
                           for hs in groups])
        q = q * lax.rsqrt(jnp.sum(q * q, -1, keepdims=True) + 1e-6) * (DN_DK ** -0.5)
        k = k * lax.rsqrt(jnp.sum(k * k, -1, keepdims=True) + 1e-6)
        decay = jnp.exp(jnp.where(causal, gcol - grow, -jnp.inf))
        kb = k * bcol
        a_mat = jnp.where(strict, _bdot_nt(kb, k) * decay, 0.0)
        t_low = _unit_lower_inverse_minus_eye(a_mat, n, c)
        eg = jnp.exp(gcol)
        rhs = jnp.concatenate([v * bcol, kb * eg], axis=2)
        sol = rhs + _bdot_3x(_split(t_low), _split(rhs))
        u_base = sol[:, :, :DN_DV]
        w_cum = sol[:, :, DN_DV:].astype(BF16)
        a_qk = _bdot_nt(q, k) * decay
        q_dec = (q * eg).astype(BF16)
        k_dec = (k * jnp.exp(glast - gcol)).astype(BF16)

        s_old = [[s_ref[h] for h in hs] for hs in groups]
        ws, qs = [], []
        for gi, hs in enumerate(groups):
            wg, qg = [], []
            for j in range(DN_STACK):
                rows = slice(j * c, (j + 1) * c)
                lhs = jnp.concatenate([w_cum[gi, rows], q_dec[gi, rows]], axis=0)
                r = jnp.dot(lhs, s_old[gi][j].astype(BF16), preferred_element_type=F32)
                wg.append(r[:c])
                qg.append(r[c:])
            ws.append(jnp.concatenate(wg, axis=0))
            qs.append(jnp.concatenate(qg, axis=0))
        u = u_base - jnp.stack(ws)
        u_b = u.astype(BF16)
        o = jnp.stack(qs) + _bdot(a_qk, u_b)
        for gi, hs in enumerate(groups):
            for j, h in enumerate(hs):
                rows = slice(j * c, (j + 1) * c)
                g_tot = jnp.exp(g_cum[c - 1:c, h:h + 1])
                s_ref[h] = s_old[gi][j] * g_tot + _dot_tn(k_dec[gi, rows], u_b[gi, rows])
        o = o * lax.rsqrt(jnp.mean(o * o, -1, keepdims=True) + EPS) * nw_ref[...]
        z = stack(zz)
        o = o * (z * jax.nn.sigmoid(z))
        for gi, hs in enumerate(groups):
            for j, h in enumerate(hs):
                o_ref[:, h * DN_DV:(h + 1) * DN_DV] = o[gi, j * c:(j + 1) * c].astype(o_ref.dtype)

    @pl.when(ch == nch - 1)
    def _():
        sout_ref[0] = s_ref[...]


def _deltanet(proj, gates, conv_init, s0, conv_w, a_log, dt_bias, norm_w, *, nb, t, c, t_valid, row0):
    w = DN_QK_W
    nch = t // c
    cur_row = lambda b, k: row0 // c + b * nch + k
    prev_row = lambda b, k: jnp.maximum(row0 // SUBLANES + (b * t + k * c) // SUBLANES - 1, 0)
    kq, kk, kv, kz = (OFF_QKV // w, (OFF_QKV + DN_QK_W) // w, (OFF_QKV + 2 * DN_QK_W) // w, OFF_Z // w)

    def cur(col):
        return pl.BlockSpec((c, w), lambda b, k: (cur_row(b, k), col))

    def prev(col):
        return pl.BlockSpec((SUBLANES, w), lambda b, k: (prev_row(b, k), col))

    def init(col):
        return pl.BlockSpec((1, SUBLANES, w), lambda b, k: (b, 0, col))

    def taps(col):
        return pl.BlockSpec((DN_CONV, w), lambda b, k: (0, col))

    row = lambda: pl.BlockSpec((1, LANES), lambda b, k: (0, 0))
    pad = lambda x: jnp.zeros((1, LANES), F32).at[0, :DN_HEADS].set(x.astype(F32))
    state = pl.BlockSpec((1, DN_HEADS, DN_DK, DN_DV), lambda b, k: (b, 0, 0, 0))
    o, s_new = pl.pallas_call(
        functools.partial(_dn_kernel, c=c, t_valid=t_valid),
        out_shape=(jax.ShapeDtypeStruct((nb * t, DN_V_W), BF16),
                   jax.ShapeDtypeStruct((nb, DN_HEADS, DN_DK, DN_DV), F32)),
        grid=(nb, nch),
        in_specs=[cur(kq), cur(kk), cur(kv), prev(kq), prev(kk), prev(kv),
                  init(kq), init(kk), init(kv), cur(kz),
                  pl.BlockSpec((c, LANES), lambda b, k: (cur_row(b, k), 0)),
                  pl.BlockSpec((c, LANES), lambda b, k: (cur_row(b, k), 1)),
                  taps(kq), taps(kk), taps(kv), row(), row(), row(), state],
        out_specs=(pl.BlockSpec((c, w), lambda b, k: (b * nch + k, 0)), state),
        scratch_shapes=[pltpu.VMEM((DN_HEADS, DN_DK, DN_DV), F32),
                        pltpu.VMEM((c + SUBLANES, w), F32)],
        compiler_params=_cparams(("parallel", "arbitrary")),
        name="deltanet",
    )(proj, proj, proj, proj, proj, proj, conv_init, conv_init, conv_init, proj, gates, gates,
      conv_w, conv_w, conv_w, pad(a_log), pad(dt_bias), norm_w.reshape(1, DN_DV).astype(F32), s0)
    return o, s_new


def _swa_kernel(slopes_ref, sinks_ref, q_ref, kp_ref, kc_ref, vp_ref, vc_ref, o_ref, *, tq, first_prev_valid):
    blk = pl.program_id(1)
    pair = pl.program_id(2)
    tk = WINDOW + tq
    n = SW_GROUP * tq
    k = jnp.concatenate([kp_ref[...], kc_ref[...]], axis=0)
    v = jnp.concatenate([vp_ref[...], vc_ref[...]], axis=0)
    k_swap = pltpu.roll(k, SW_HD, 1)
    v_swap = pltpu.roll(v, SW_HD, 1)
    low_kv = lax.broadcasted_iota(jnp.int32, (tk, LANES), 1) < SW_HD
    q = q_ref[...]
    lane_low = lax.broadcasted_iota(jnp.int32, (n, LANES), 1) < SW_HD
    grp = lax.broadcasted_iota(jnp.int32, (n, 1), 0) // tq
    own_half = lane_low == ((grp % 2) == 0)
    low_out = lax.broadcasted_iota(jnp.int32, (tq, LANES), 1) < SW_HD
    ri = lax.broadcasted_iota(jnp.int32, (n, tk), 0)
    ci = lax.broadcasted_iota(jnp.int32, (n, tk), 1)
    dist = (ri % tq) + WINDOW - ci
    valid = (dist >= 0) & (dist <= WINDOW)
    if not first_prev_valid:
        valid = valid & ((ci >= WINDOW) | (blk > 0))
    distf = dist.astype(F32)
    for kh in range(SW_PAIR):
        head0 = (pair * SW_PAIR + kh) * SW_GROUP
        slope = jnp.zeros((n, 1), F32)
        sink = jnp.zeros((n, 1), F32)
        for g in range(SW_GROUP):
            slope = jnp.where(grp == g, slopes_ref[head0 + g], slope)
            sink = jnp.where(grp == g, sinks_ref[head0 + g], sink)
        kk = (jnp.where(low_kv, k, k_swap) if kh == 0 else jnp.where(low_kv, k_swap, k)).astype(BF16)
        vv = (jnp.where(low_kv, v, v_swap) if kh == 0 else jnp.where(low_kv, v_swap, v)).astype(BF16)
        blocks = [kh * (SW_GROUP // 2) + g // 2 for g in range(SW_GROUP)]
        qs = jnp.concatenate([q[:, b * LANES:(b + 1) * LANES] for b in blocks], axis=0)
        qm = jnp.where(own_half, qs, 0.0).astype(BF16)
        s = _dot_nt(qm, kk) * (SW_HD ** -0.5)
        s = jnp.where(valid, s - slope * distf, -jnp.inf)
        m = jnp.maximum(jnp.max(s, -1, keepdims=True), sink)
        p = jnp.exp(s - m)
        p = p / (jnp.sum(p, -1, keepdims=True) + jnp.exp(sink - m))
        o = jnp.dot(p.astype(BF16), vv, preferred_element_type=F32)
        for j in range(SW_GROUP // 2):
            even = o[(2 * j) * tq:(2 * j + 1) * tq]
            odd = o[(2 * j + 1) * tq:(2 * j + 2) * tq]
            b = kh * (SW_GROUP // 2) + j
            o_ref[:, b * LANES:(b + 1) * LANES] = jnp.where(low_out, even, odd).astype(o_ref.dtype)


def _swa(proj, k_prev_src, v_prev_src, slopes, sinks, *, nb, nblk, tq, row0, prev_from_proj):
    qw = SW_PAIR * SW_GROUP * SW_HD
    kw = SW_PAIR * SW_HD
    qb, kb, vb = OFF_SQ // qw, OFF_SK // kw, OFF_SV // kw
    cur_row = lambda b, n, p: row0 // tq + b * nblk + n
    if prev_from_proj:
        prev_k = pl.BlockSpec((WINDOW, kw), lambda b, n, p: (b * nblk + jnp.maximum(n - 1, 0), kb + p))
        prev_v = pl.BlockSpec((WINDOW, kw), lambda b, n, p: (b * nblk + jnp.maximum(n - 1, 0), vb + p))
    else:
        prev_k = pl.BlockSpec((WINDOW, kw), lambda b, n, p: (b, p))
        prev_v = pl.BlockSpec((WINDOW, kw), lambda b, n, p: (b, p))
    smem = pl.BlockSpec(memory_space=pltpu.SMEM)
    return pl.pallas_call(
        functools.partial(_swa_kernel, tq=tq, first_prev_valid=not prev_from_proj),
        out_shape=jax.ShapeDtypeStruct((nb * nblk * tq, SW_Q_W), BF16),
        grid=(nb, nblk, SW_KV_HEADS // SW_PAIR),
        in_specs=[smem, smem,
                  pl.BlockSpec((tq, qw), lambda b, n, p: (cur_row(b, n, p), qb + p)),
                  prev_k,
                  pl.BlockSpec((tq, kw), lambda b, n, p: (cur_row(b, n, p), kb + p)),
                  prev_v,
                  pl.BlockSpec((tq, kw), lambda b, n, p: (cur_row(b, n, p), vb + p))],
        out_specs=pl.BlockSpec((tq, qw), lambda b, n, p: (b * nblk + n, p)),
        compiler_params=_cparams(("parallel", "parallel", "parallel")),
        name="swa",
    )(slopes, sinks, proj, k_prev_src, proj, v_prev_src, proj)


def _prep_in_proj(w_in):
    sw_part = w_in[:, :, ORIG_SQ:]
    side = jnp.zeros(w_in.shape[:2] + (2 * LANES,), BF16)
    side = side.at[:, :, :DN_HEADS].set(w_in[:, :, ORIG_B:ORIG_A].astype(BF16))
    side = side.at[:, :, LANES:LANES + DN_HEADS].set(w_in[:, :, ORIG_A:ORIG_SQ].astype(BF16))
    return sw_part, side


def _ffn_half_step(x, ln, wg, wu, wd, layer, name):
    h = _rmsnorm(x, ln, BF16)
    hidden = _swiglu_up(h, wg, wu, layer)
    return _residual_matmul(hidden, wd, layer, x, 0.5, DOWN_TK, TN_HALF, name)


def _rows(a, r0, r1, c0, c1):
    return lax.slice(a, (r0, c0), (r1, c1))


def kernel(x_prompt, x_sample, state_dn_conv, state_dn_recurrent, cache_swa_k, cache_swa_v, ln_ffn1, w_ffn1_gate, w_ffn1_up, w_ffn1_down, ln_mix, w_in, dn_conv_w, dn_a_log, dn_dt_bias, dn_norm_w, swa_sinks, w_branch_a, w_branch_b, w_out, ln_ffn2, w_ffn2_gate, w_ffn2_up, w_ffn2_down, ln_final):
    slopes = 2.0 ** (-8.0 * jnp.arange(1, SW_HEADS + 1, dtype=F32) / SW_HEADS)
    xs = jnp.pad(x_sample, ((0, 0), (0, DEC_PAD - DEC_SEQ), (0, 0)))
    x = jnp.concatenate([x_prompt.reshape(M_PROMPT, D_MODEL), xs.reshape(M_SAMPLE, D_MODEL)], axis=0)
    zero_conv = jnp.zeros((BATCH, SUBLANES, DN_CONV_DIM), F32)
    zero_state = jnp.zeros((BATCH, DN_HEADS, DN_DK, DN_DV), F32)

    wg1, wu1, wd1 = w_ffn1_gate, w_ffn1_up, w_ffn1_down.astype(BF16)
    wg2, wu2, wd2 = w_ffn2_gate, w_ffn2_up, w_ffn2_down.astype(BF16)
    w_sw, w_side = _prep_in_proj(w_in)
    w_a = w_branch_a.astype(BF16)
    w_b = w_branch_b.astype(BF16)
    w_o = w_out.astype(BF16)
    buf = cache_swa_k.shape[2]
    keep = min(WINDOW, SEQ)

    outs = [[] for _ in range(8)]
    for l in range(DEPTH):
        x = _ffn_half_step(x, ln_ffn1[l], wg1, wu1, wd1, l, "ffn1_down")

        h = _rmsnorm(x, ln_mix[l], BF16)
        proj_dn = _matmul_cast(h, w_in, l, DN_PART_W, F32, TN_HALF, "in_proj_dn")
        proj_sw = _matmul_cast(h, w_sw, l, SW_PART_W, F32, TN_HALF, "in_proj_sw")
        gates = _matmul(h, w_side, l, F32, 2 * LANES, "in_proj_gates")

        conv_s = jnp.pad(state_dn_conv[l], ((0, 0), (SUBLANES - (DN_CONV - 1), 0), (0, 0)))
        dn_args = (dn_conv_w[l], dn_a_log[l], dn_dt_bias[l], dn_norm_w[l])
        oa_p, rec_p = _deltanet(proj_dn, gates, zero_conv, zero_state, *dn_args,
                                nb=BATCH, t=SEQ, c=DN_CHUNK, t_valid=DN_CHUNK, row0=0)
        oa_s, rec_s = _deltanet(proj_dn, gates, conv_s, state_dn_recurrent[l].astype(F32), *dn_args,
                                nb=DEC_BATCH, t=DEC_PAD, c=DEC_PAD, t_valid=DEC_SEQ, row0=M_PROMPT)

        ck = cache_swa_k[l].reshape(DEC_BATCH * buf, SW_KV_W)
        cv = cache_swa_v[l].reshape(DEC_BATCH * buf, SW_KV_W)
        sinks = swa_sinks[l].astype(F32)
        ob_p = _swa(proj_sw, proj_sw, proj_sw, slopes, sinks, nb=BATCH, nblk=SEQ // WINDOW,
                    tq=WINDOW, row0=0, prev_from_proj=True)
        ob_s = _swa(proj_sw, ck, cv, slopes, sinks, nb=DEC_BATCH, nblk=1,
                    tq=DEC_PAD, row0=M_PROMPT, prev_from_proj=False)

        o_a = jnp.concatenate([oa_p, oa_s], axis=0)
        o_b = jnp.concatenate([ob_p, ob_s], axis=0)
        merged = _branch_merge(o_a, o_b, w_a, w_b, l, proj_sw)
        x = _residual_matmul(merged, w_o, l, x, 1.0, OUT_TK, TN, "out_proj")

        x = _ffn_half_step(x, ln_ffn2[l], wg2, wu2, wd2, l, "ffn2_down")

        last = [(b + 1) * SEQ for b in range(BATCH)]
        sample = lambda p, c0, c1: _rows(p, M_PROMPT, M_ROWS, c0, c1).reshape(DEC_BATCH, DEC_PAD, c1 - c0)[:, :DEC_SEQ]
        outs[0].append(jnp.stack([_rows(proj_dn, e - (DN_CONV - 1), e, OFF_QKV, OFF_QKV + DN_CONV_DIM) for e in last]))
        outs[1].append(rec_p)
        outs[2].append(jnp.stack([_rows(proj_sw, e - keep, e, OFF_SK, OFF_SK + SW_KV_W) for e in last])
                       .reshape(BATCH, keep, SW_KV_HEADS, SW_HD))
        outs[3].append(jnp.stack([_rows(proj_sw, e - keep, e, OFF_SV, OFF_SV + SW_KV_W) for e in last])
                       .reshape(BATCH, keep, SW_KV_HEADS, SW_HD))
        outs[4].append(sample(proj_dn, OFF_QKV, OFF_QKV + DN_CONV_DIM)[:, DEC_SEQ - (DN_CONV - 1):])
        outs[5].append(rec_s)
        new_k = sample(proj_sw, OFF_SK, OFF_SK + SW_KV_W).reshape(DEC_BATCH, DEC_SEQ, SW_KV_HEADS, SW_HD)
        new_v = sample(proj_sw, OFF_SV, OFF_SV + SW_KV_W).reshape(DEC_BATCH, DEC_SEQ, SW_KV_HEADS, SW_HD)
        outs[6].append(jnp.concatenate([cache_swa_k[l], new_k], axis=1)[:, -buf:])
        outs[7].append(jnp.concatenate([cache_swa_v[l], new_v], axis=1)[:, -buf:])

    y_prompt = _rmsnorm(x, ln_final, F32, 0, M_PROMPT).reshape(BATCH, SEQ, D_MODEL)
    y_sample = _rmsnorm(x, ln_final, F32, M_PROMPT, M_SAMPLE).reshape(DEC_BATCH, DEC_PAD, D_MODEL)[:, :DEC_SEQ]
    rec_dtype = state_dn_recurrent.dtype
    return (y_prompt, y_sample,
            jnp.stack(outs[0]), jnp.stack(outs[1]).astype(rec_dtype), jnp.stack(outs[2]), jnp.stack(outs[3]),
            jnp.stack(outs[4]), jnp.stack(outs[5]).astype(rec_dtype), jnp.stack(outs[6]), jnp.stack(outs[7]))
```

```python
import functools
import math

import jax
import jax.numpy as jnp
from jax import lax
from jax.experimental import pallas as pl
from jax.experimental.pallas import tpu as pltpu

F32 = jnp.float32
BF16 = jnp.bfloat16
HIGHEST = lax.Precision.HIGHEST

D_MODEL = 4096
BATCH = 4
SEQ = 2048
DEPTH = 2
DEC_BATCH = 32
DEC_SEQ = 4
DN_HEADS = 16
DN_DK = 128
DN_DV = 128
DN_CONV = 4
DN_CHUNK = 64
DN_QK_W = DN_HEADS * DN_DK
DN_V_W = DN_HEADS * DN_DV
DN_CONV_DIM = 2 * DN_QK_W + DN_V_W
SW_HEADS = 32
SW_KV_HEADS = 8
SW_GROUP = SW_HEADS // SW_KV_HEADS
SW_HD = 64
SW_Q_W = SW_HEADS * SW_HD
SW_KV_W = SW_KV_HEADS * SW_HD
WINDOW = 128
D_FF = 11008
EPS = 1e-6

SUBLANES = 8
LANES = 128
VMEM_LIMIT = 56 * 1024 * 1024

DEC_PAD = SUBLANES
M_PROMPT = BATCH * SEQ
M_SAMPLE = DEC_BATCH * DEC_PAD
M_ROWS = M_PROMPT + M_SAMPLE
TM = M_ROWS // 8
TR = 256
TN = 1024
TN_HALF = 512

OFF_QKV = 0
OFF_Z = OFF_QKV + DN_CONV_DIM
DN_PART_W = OFF_Z + DN_V_W
OFF_SQ = 0
OFF_SK = OFF_SQ + SW_Q_W
OFF_SV = OFF_SK + SW_KV_W
OFF_GA = OFF_SV + SW_KV_W
OFF_GB = OFF_GA + D_MODEL
SW_PART_W = OFF_GB + D_MODEL
ORIG_B = DN_PART_W
ORIG_A = ORIG_B + DN_HEADS
ORIG_SQ = ORIG_A + DN_HEADS

FF_TN = 256
DOWN_TK = D_FF // 2
OUT_TK = D_MODEL // 2

DN_STACK = 4
DN_GROUP_BATCH = 4
SW_PAIR = 2


def _cparams(sem):
    return pltpu.CompilerParams(dimension_semantics=sem, vmem_limit_bytes=VMEM_LIMIT)


def _row_rsqrt(ssq_ref, d):
    return lax.rsqrt(ssq_ref[:, 0:1] * (1.0 / d) + EPS)


def _prenorm_kernel(x_ref, w_ref, xs_ref, ssq_ref):
    x = x_ref[...]
    xs_ref[...] = (x * w_ref[...]).astype(xs_ref.dtype)
    ssq_ref[...] = jnp.broadcast_to(jnp.sum(x * x, -1, keepdims=True), ssq_ref.shape)


def _prenorm(x, w):
    m, d = x.shape
    return pl.pallas_call(
        _prenorm_kernel,
        out_shape=(jax.ShapeDtypeStruct((m, d), BF16), jax.ShapeDtypeStruct((m, LANES), F32)),
        grid=(m // TR,),
        in_specs=[pl.BlockSpec((TR, d), lambda i: (i, 0)), pl.BlockSpec((1, d), lambda i: (0, 0))],
        out_specs=(pl.BlockSpec((TR, d), lambda i: (i, 0)), pl.BlockSpec((TR, LANES), lambda i: (i, 0))),
        compiler_params=_cparams(("parallel",)),
        name="prenorm",
    )(x, w.reshape(1, d))


def _rmsnorm_kernel(x_ref, w_ref, o_ref):
    x = x_ref[...]
    y = x * lax.rsqrt(jnp.mean(x * x, -1, keepdims=True) + EPS)
    o_ref[...] = (y * w_ref[...]).astype(o_ref.dtype)


def _rmsnorm(x, w, out_dtype, row0=0, nrows=None):
    d = x.shape[1]
    nrows = x.shape[0] if nrows is None else nrows
    blk0 = row0 // TR
    return pl.pallas_call(
        _rmsnorm_kernel,
        out_shape=jax.ShapeDtypeStruct((nrows, d), out_dtype),
        grid=(nrows // TR,),
        in_specs=[pl.BlockSpec((TR, d), lambda i: (blk0 + i, 0)), pl.BlockSpec((1, d), lambda i: (0, 0))],
        out_specs=pl.BlockSpec((TR, d), lambda i: (i, 0)),
        compiler_params=_cparams(("parallel",)),
        name="rmsnorm",
    )(x, w.reshape(1, d))


def _mm_kernel(a_ref, ssq_ref, w_ref, o_ref):
    d = jnp.dot(a_ref[...], w_ref[...], preferred_element_type=F32)
    o_ref[...] = (d * _row_rsqrt(ssq_ref, a_ref.shape[1])).astype(o_ref.dtype)


def _normed_matmul(xs, ssq, w, layer, col0, ncols, out_dtype, tn, name):
    m, k = xs.shape
    blk0 = col0 // tn
    return pl.pallas_call(
        _mm_kernel,
        out_shape=jax.ShapeDtypeStruct((m, ncols), out_dtype),
        grid=(m // TM, ncols // tn),
        in_specs=[pl.BlockSpec((TM, k), lambda i, j: (i, 0)),
                  pl.BlockSpec((TM, LANES), lambda i, j: (i, 0)),
                  pl.BlockSpec((None, k, tn), lambda i, j: (layer, 0, blk0 + j))],
        out_specs=pl.BlockSpec((TM, tn), lambda i, j: (i, j)),
        compiler_params=_cparams(("parallel", "parallel")),
        name=name,
    )(xs, ssq, w)


def _swiglu_kernel(a_ref, ssq_ref, wg_ref, wu_ref, o_ref, wgb_ref, wub_ref):
    @pl.when(pl.program_id(1) == 0)
    def _():
        wgb_ref[...] = wg_ref[...].astype(BF16)
        wub_ref[...] = wu_ref[...].astype(BF16)

    a = a_ref[...]
    r = _row_rsqrt(ssq_ref, a_ref.shape[1])
    g = jnp.dot(a, wgb_ref[...], preferred_element_type=F32) * r
    u = jnp.dot(a, wub_ref[...], preferred_element_type=F32) * r
    o_ref[...] = (g * jax.nn.sigmoid(g) * u).astype(o_ref.dtype)


def _swiglu_up(xs, ssq, wg, wu, layer):
    m, k = xs.shape
    n = wg.shape[2]
    tn = FF_TN
    wspec = pl.BlockSpec((None, k, tn), lambda j, i: (layer, 0, j))
    return pl.pallas_call(
        _swiglu_kernel,
        out_shape=jax.ShapeDtypeStruct((m, n), BF16),
        grid=(n // tn, m // TM),
        in_specs=[pl.BlockSpec((TM, k), lambda j, i: (i, 0)),
                  pl.BlockSpec((TM, LANES), lambda j, i: (i, 0)), wspec, wspec],
        out_specs=pl.BlockSpec((TM, tn), lambda j, i: (i, j)),
        scratch_shapes=[pltpu.VMEM((k, tn), BF16), pltpu.VMEM((k, tn), BF16)],
        compiler_params=_cparams(("parallel", "arbitrary")),
        name="ffn_gate_up",
    )(xs, ssq, wg, wu)


def _residual_mm_kernel(a_ref, w_ref, r_ref, ln_ref, o_ref, xs_ref, ssq_ref, *, scale, nk):
    j = pl.program_id(1)
    kk = pl.program_id(2)
    d = jnp.dot(a_ref[...], w_ref[...], preferred_element_type=F32)
    if scale != 1.0:
        d = d * scale

    def finish(val):
        o_ref[...] = val
        xs_ref[...] = (val * ln_ref[...]).astype(xs_ref.dtype)
        part = jnp.broadcast_to(jnp.sum(val * val, -1, keepdims=True), ssq_ref.shape)

        @pl.when(j == 0)
        def _():
            ssq_ref[...] = part

        @pl.when(j != 0)
        def _():
            ssq_ref[...] += part

    if nk == 1:
        finish(r_ref[...] + d)
    else:
        @pl.when(kk == 0)
        def _():
            o_ref[...] = r_ref[...] + d

        @pl.when((kk > 0) & (kk < nk - 1))
        def _():
            o_ref[...] += d

        @pl.when(kk == nk - 1)
        def _():
            finish(o_ref[...] + d)


def _residual_matmul(a, w, layer, res, ln_next, scale, tk, tn, name):
    m, k = a.shape
    n = w.shape[2]
    nk = k // tk
    return pl.pallas_call(
        functools.partial(_residual_mm_kernel, scale=scale, nk=nk),
        out_shape=(jax.ShapeDtypeStruct((m, n), F32), jax.ShapeDtypeStruct((m, n), BF16),
                   jax.ShapeDtypeStruct((m, LANES), F32)),
        grid=(m // TM, n // tn, nk),
        in_specs=[pl.BlockSpec((TM, tk), lambda i, j, kk: (i, kk)),
                  pl.BlockSpec((None, tk, tn), lambda i, j, kk: (layer, kk, j)),
                  pl.BlockSpec((TM, tn), lambda i, j, kk: (i, j)),
                  pl.BlockSpec((1, tn), lambda i, j, kk: (0, j))],
        out_specs=(pl.BlockSpec((TM, tn), lambda i, j, kk: (i, j)),
                   pl.BlockSpec((TM, tn), lambda i, j, kk: (i, j)),
                   pl.BlockSpec((TM, LANES), lambda i, j, kk: (i, 0))),
        compiler_params=_cparams(("parallel", "arbitrary", "arbitrary")),
        name=name,
    )(a, w, res, ln_next.reshape(1, n))


def _merge_kernel(oa_ref, ob_ref, wa_ref, wb_ref, ga_ref, gb_ref, o_ref):
    ya = jnp.dot(oa_ref[...], wa_ref[...], preferred_element_type=F32)
    yb = jnp.dot(ob_ref[...], wb_ref[...], preferred_element_type=F32)
    o = jax.nn.sigmoid(ga_ref[...]) * ya + jax.nn.sigmoid(gb_ref[...]) * yb
    o_ref[...] = o.astype(o_ref.dtype)


def _branch_merge(o_a, o_b, w_a, w_b, layer, proj):
    m, ka = o_a.shape
    kb = o_b.shape[1]
    n = w_a.shape[2]
    tn = TN_HALF
    ga_blk = OFF_GA // tn
    gb_blk = OFF_GB // tn
    return pl.pallas_call(
        _merge_kernel,
        out_shape=jax.ShapeDtypeStruct((m, n), BF16),
        grid=(m // TM, n // tn),
        in_specs=[pl.BlockSpec((TM, ka), lambda i, j: (i, 0)),
                  pl.BlockSpec((TM, kb), lambda i, j: (i, 0)),
                  pl.BlockSpec((None, ka, tn), lambda i, j: (layer, 0, j)),
                  pl.BlockSpec((None, kb, tn), lambda i, j: (layer, 0, j)),
                  pl.BlockSpec((TM, tn), lambda i, j: (i, ga_blk + j)),
                  pl.BlockSpec((TM, tn), lambda i, j: (i, gb_blk + j))],
        out_specs=pl.BlockSpec((TM, tn), lambda i, j: (i, j)),
        compiler_params=_cparams(("parallel", "parallel")),
        name="branch_merge",
    )(o_a, o_b, w_a, w_b, proj, proj)


def _dot_nt(a, b, precision=None):
    return lax.dot_general(a, b, (((1,), (1,)), ((), ())), precision=precision,
                           preferred_element_type=F32)


def _dot_tn(a, b):
    return lax.dot_general(a, b, (((0,), (0,)), ((), ())), preferred_element_type=F32)


def _dot_hi(a, b):
    return jnp.dot(a, b, precision=HIGHEST, preferred_element_type=F32)


def _split(x):
    hi = x.astype(BF16)
    return hi, (x - hi.astype(F32)).astype(BF16)


def _bdot(a, b):
    return jnp.einsum("gij,gjk->gik", a.astype(BF16), b.astype(BF16), preferred_element_type=F32)


def _bdot_nt(a, b):
    return jnp.einsum("gik,gjk->gij", a.astype(BF16), b.astype(BF16), preferred_element_type=F32)


def _bdot_3x(a, b):
    return _bdot(a[0], b[0]) + _bdot(a[0], b[1]) + _bdot(a[1], b[0])


def _unit_lower_inverse_minus_eye(a, n, c):
    ri = lax.broadcasted_iota(jnp.int32, (n, n), 0)
    ci = lax.broadcasted_iota(jnp.int32, (n, n), 1)
    base = min(16, c)
    p = jnp.where((ri // base) == (ci // base), a, 0.0)
    nt = -p
    for _ in range(int(math.log2(base)) - 1):
        p = _bdot(p, p)
        nt = nt + p + _bdot(nt, p)
    blk = base
    while blk < c:
        pair = (ri // (2 * blk)) == (ci // (2 * blk))
        lower = pair & (((ri // blk) % 2) == 1) & (((ci // blk) % 2) == 0)
        l = jnp.where(lower, a, 0.0)
        lt = l + _bdot(l, nt)
        nt = nt - lt - _bdot(nt, lt)
        blk *= 2
    return nt


def _dn_kernel(qc_ref, kc_ref, vc_ref, qp_ref, kp_ref, vp_ref, qi_ref, ki_ref, vi_ref,
               z_ref, b_ref, a_ref, wq_ref, wk_ref, wv_ref, alog_ref, dtb_ref, nw_ref, s0_ref,
               o_ref, sout_ref, s_ref, ext_ref, *, c, t_valid):
    ch = pl.program_id(1)
    nch = pl.num_programs(1)
    n = DN_STACK * c

    @pl.when(ch == 0)
    def _():
        s_ref[...] = s0_ref[0]

    def conv(cur_ref, prev_ref, init_ref, w_ref):
        prev = jnp.where(ch == 0, init_ref[0], prev_ref[...])
        ext_ref[0:SUBLANES, :] = prev
        ext_ref[SUBLANES:SUBLANES + c, :] = cur_ref[...]
        first = SUBLANES - (DN_CONV - 1)
        y = ext_ref[first:first + c, :] * w_ref[0:1, :]
        for j in range(1, DN_CONV):
            y = y + ext_ref[first + j:first + j + c, :] * w_ref[j:j + 1, :]
        return y * jax.nn.sigmoid(y)

    yq = conv(qc_ref, qp_ref, qi_ref, wq_ref)
    yk = conv(kc_ref, kp_ref, ki_ref, wk_ref)
    yv = conv(vc_ref, vp_ref, vi_ref, wv_ref)
    zz = z_ref[...]

    beta_all = jax.nn.sigmoid(b_ref[...])
    araw = a_ref[...] + dtb_ref[...]
    softplus = jnp.maximum(araw, 0.0) + jnp.log1p(jnp.exp(-jnp.abs(araw)))
    g_all = -jnp.exp(alog_ref[...]) * softplus
    if t_valid < c:
        row_ok = lax.broadcasted_iota(jnp.int32, (c, LANES), 0) < t_valid
        beta_all = jnp.where(row_ok, beta_all, 0.0)
        g_all = jnp.where(row_ok, g_all, 0.0)
    tri = lax.broadcasted_iota(jnp.int32, (c, c), 0) >= lax.broadcasted_iota(jnp.int32, (c, c), 1)
    g_cum = _dot_hi(tri.astype(F32), g_all)
    lri = lax.broadcasted_iota(jnp.int32, (LANES, LANES), 0)
    lci = lax.broadcasted_iota(jnp.int32, (LANES, LANES), 1)
    g_cum_t = _dot_nt((lri == lci).astype(F32), g_cum, HIGHEST)

    ri = lax.broadcasted_iota(jnp.int32, (n, n), 0)
    ci = lax.broadcasted_iota(jnp.int32, (n, n), 1)
    same_head = (ri // c) == (ci // c)
    causal = same_head & ((ri % c) >= (ci % c))
    strict = same_head & ((ri % c) > (ci % c))

    for g0 in range(0, DN_HEADS // DN_STACK, DN_GROUP_BATCH):
        groups = [[(g0 + gi) * DN_STACK + j for j in range(DN_STACK)] for gi in range(DN_GROUP_BATCH)]

        def stack(y):
            return jnp.stack([jnp.concatenate([y[:, h * DN_DK:(h + 1) * DN_DK] for h in hs], axis=0) for hs in groups])

        def column(m):
            return jnp.stack([jnp.concatenate([m[:, h:h + 1] for h in hs], axis=0) for hs in groups])

        q = stack(yq)
        k = stack(yk)
        v = stack(yv)
        bcol = column(beta_all)
        gcol = column(g_cum)
        grow = jnp.stack([jnp.concatenate([g_cum_t[h:h + 1, :] for h in hs], axis=1) for hs in groups])
        glast = jnp.stack([jnp.concatenate([jnp.broadcast_to(g_cum[c - 1:c, h:h + 1], (c, 1)) for h in hs], axis=0)
                           for hs in groups])
        q = q * lax.rsqrt(jnp.sum(q * q, -1, keepdims=True) + 1e-6) * (DN_DK ** -0.5)
        k = k * lax.rsqrt(jnp.sum(k * k, -1, keepdims=True) + 1e-6)
        decay = jnp.exp(jnp.where(causal, gcol - grow, -jnp.inf))
        kb = k * bcol
        a_mat = jnp.where(strict, _bdot_nt(kb, k) * decay, 0.0)
        t_low = _unit_lower_inverse_minus_eye(a_mat, n, c)
        eg = jnp.exp(gcol)
        rhs = jnp.concatenate([v * bcol, kb * eg], axis=2)
        sol = rhs + _bdot_3x(_split(t_low), _split(rhs))
        u_base = sol[:, :, :DN_DV]
        w_cum = sol[:, :, DN_DV:].astype(BF16)
        a_qk = _bdot_nt(q, k) * decay
        q_dec = (q * eg).astype(BF16)
        k_dec = (k * jnp.exp(glast - gcol)).astype(BF16)

        s_old = [[s_ref[h] for h in hs] for hs in groups]
        ws, qs = [], []
        for gi, hs in enumerate(groups):
            wg, qg = [], []
            for j in range(DN_STACK):
                rows = slice(j * c, (j + 1) * c)
                lhs = jnp.concatenate([w_cum[gi, rows], q_dec[gi, rows]], axis=0)
                r = jnp.dot(lhs, s_old[gi][j].astype(BF16), preferred_element_type=F32)
                wg.append(r[:c])
                qg.append(r[c:])
            ws.append(jnp.concatenate(wg, axis=0))
            qs.append(jnp.concatenate(qg, axis=0))
        u = u_base - jnp.stack(ws)
        u_b = u.astype(BF16)
        o = jnp.stack(qs) + _bdot(a_qk, u_b)
        for gi, hs in enumerate(groups):
            for j, h in enumerate(hs):
                rows = slice(j * c, (j + 1) * c)
                g_tot = jnp.exp(g_cum[c - 1:c, h:h + 1])
                s_ref[h] = s_old[gi][j] * g_tot + _dot_tn(k_dec[gi, rows], u_b[gi, rows])
        o = o * lax.rsqrt(jnp.mean(o * o, -1, keepdims=True) + EPS) * nw_ref[...]
        z = stack(zz)
        o = o * (z * jax.nn.sigmoid(z))
        for gi, hs in enumerate(groups):
            for j, h in enumerate(hs):
                o_ref[:, h * DN_DV:(h + 1) * DN_DV] = o[gi, j * c:(j + 1) * c].astype(o_ref.dtype)

    @pl.when(ch == nch - 1)
    def _():
        sout_ref[0] = s_ref[...]


def _deltanet(proj, gates, conv_init, s0, conv_w, a_log, dt_bias, norm_w, *, nb, t, c, t_valid, row0):
    w = DN_QK_W
    nch = t // c
    cur_row = lambda b, k: row0 // c + b * nch + k
    prev_row = lambda b, k: jnp.maximum(row0 // SUBLANES + (b * t + k * c) // SUBLANES - 1, 0)
    kq, kk, kv, kz = (OFF_QKV // w, (OFF_QKV + DN_QK_W) // w, (OFF_QKV + 2 * DN_QK_W) // w, OFF_Z // w)

    def cur(col):
        return pl.BlockSpec((c, w), lambda b, k: (cur_row(b, k), col))

    def prev(col):
        return pl.BlockSpec((SUBLANES, w), lambda b, k: (prev_row(b, k), col))

    def init(col):
        return pl.BlockSpec((1, SUBLANES, w), lambda b, k: (b, 0, col))

    def taps(col):
        return pl.BlockSpec((DN_CONV, w), lambda b, k: (0, col))

    row = lambda: pl.BlockSpec((1, LANES), lambda b, k: (0, 0))
    pad = lambda x: jnp.zeros((1, LANES), F32).at[0, :DN_HEADS].set(x.astype(F32))
    state = pl.BlockSpec((1, DN_HEADS, DN_DK, DN_DV), lambda b, k: (b, 0, 0, 0))
    o, s_new = pl.pallas_call(
        functools.partial(_dn_kernel, c=c, t_valid=t_valid),
        out_shape=(jax.ShapeDtypeStruct((nb * t, DN_V_W), BF16),
                   jax.ShapeDtypeStruct((nb, DN_HEADS, DN_DK, DN_DV), F32)),
        grid=(nb, nch),
        in_specs=[cur(kq), cur(kk), cur(kv), prev(kq), prev(kk), prev(kv),
                  init(kq), init(kk), init(kv), cur(kz),
                  pl.BlockSpec((c, LANES), lambda b, k: (cur_row(b, k), 0)),
                  pl.BlockSpec((c, LANES), lambda b, k: (cur_row(b, k), 1)),
                  taps(kq), taps(kk), taps(kv), row(), row(), row(), state],
        out_specs=(pl.BlockSpec((c, w), lambda b, k: (b * nch + k, 0)), state),
        scratch_shapes=[pltpu.VMEM((DN_HEADS, DN_DK, DN_DV), F32),
                        pltpu.VMEM((c + SUBLANES, w), F32)],
        compiler_params=_cparams(("parallel", "arbitrary")),
        name="deltanet",
    )(proj, proj, proj, proj, proj, proj, conv_init, conv_init, conv_init, proj, gates, gates,
      conv_w, conv_w, conv_w, pad(a_log), pad(dt_bias), norm_w.reshape(1, DN_DV).astype(F32), s0)
    return o, s_new


def _swa_kernel(slopes_ref, sinks_ref, q_ref, kp_ref, kc_ref, vp_ref, vc_ref, o_ref, *, tq, first_prev_valid):
    blk = pl.program_id(1)
    pair = pl.program_id(2)
    tk = WINDOW + tq
    n = SW_GROUP * tq
    k = jnp.concatenate([kp_ref[...], kc_ref[...]], axis=0)
    v = jnp.concatenate([vp_ref[...], vc_ref[...]], axis=0)
    k_swap = pltpu.roll(k, SW_HD, 1)
    v_swap = pltpu.roll(v, SW_HD, 1)
    low_kv = lax.broadcasted_iota(jnp.int32, (tk, LANES), 1) < SW_HD
    q = q_ref[...]
    lane_low = lax.broadcasted_iota(jnp.int32, (n, LANES), 1) < SW_HD
    grp = lax.broadcasted_iota(jnp.int32, (n, 1), 0) // tq
    own_half = lane_low == ((grp % 2) == 0)
    low_out = lax.broadcasted_iota(jnp.int32, (tq, LANES), 1) < SW_HD
    ri = lax.broadcasted_iota(jnp.int32, (n, tk), 0)
    ci = lax.broadcasted_iota(jnp.int32, (n, tk), 1)
    dist = (ri % tq) + WINDOW - ci
    valid = (dist >= 0) & (dist <= WINDOW)
    if not first_prev_valid:
        valid = valid & ((ci >= WINDOW) | (blk > 0))
    neg_dist = jnp.where(valid, -dist.astype(F32), -jnp.inf)
    for kh in range(SW_PAIR):
        head0 = (pair * SW_PAIR + kh) * SW_GROUP
        slope = jnp.zeros((n, 1), F32)
        sink = jnp.zeros((n, 1), F32)
        for g in range(SW_GROUP):
            slope = jnp.where(grp == g, slopes_ref[head0 + g], slope)
            sink = jnp.where(grp == g, sinks_ref[head0 + g], sink)
        kk = (jnp.where(low_kv, k, k_swap) if kh == 0 else jnp.where(low_kv, k_swap, k)).astype(BF16)
        vv = (jnp.where(low_kv, v, v_swap) if kh == 0 else jnp.where(low_kv, v_swap, v)).astype(BF16)
        blocks = [kh * (SW_GROUP // 2) + g // 2 for g in range(SW_GROUP)]
        qs = jnp.concatenate([q[:, b * LANES:(b + 1) * LANES] for b in blocks], axis=0)
        qm = jnp.where(own_half, qs * (SW_HD ** -0.5), 0.0).astype(BF16)
        s = _dot_nt(qm, kk) + slope * neg_dist
        m = jnp.maximum(jnp.max(s, -1, keepdims=True), sink)
        p = jnp.exp(s - m)
        p = p / (jnp.sum(p, -1, keepdims=True) + jnp.exp(sink - m))
        o = jnp.dot(p.astype(BF16), vv, preferred_element_type=F32)
        for j in range(SW_GROUP // 2):
            even = o[(2 * j) * tq:(2 * j + 1) * tq]
            odd = o[(2 * j + 1) * tq:(2 * j + 2) * tq]
            b = kh * (SW_GROUP // 2) + j
            o_ref[:, b * LANES:(b + 1) * LANES] = jnp.where(low_out, even, odd).astype(o_ref.dtype)


def _swa(proj, k_prev_src, v_prev_src, slopes, sinks, *, nb, nblk, tq, row0, prev_from_proj):
    qw = SW_PAIR * SW_GROUP * SW_HD
    kw = SW_PAIR * SW_HD
    qb, kb, vb = OFF_SQ // qw, OFF_SK // kw, OFF_SV // kw
    cur_row = lambda b, n, p: row0 // tq + b * nblk + n
    if prev_from_proj:
        prev_k = pl.BlockSpec((WINDOW, kw), lambda b, n, p: (b * nblk + jnp.maximum(n - 1, 0), kb + p))
        prev_v = pl.BlockSpec((WINDOW, kw), lambda b, n, p: (b * nblk + jnp.maximum(n - 1, 0), vb + p))
    else:
        prev_k = pl.BlockSpec((WINDOW, kw), lambda b, n, p: (b, p))
        prev_v = pl.BlockSpec((WINDOW, kw), lambda b, n, p: (b, p))
    smem = pl.BlockSpec(memory_space=pltpu.SMEM)
    return pl.pallas_call(
        functools.partial(_swa_kernel, tq=tq, first_prev_valid=not prev_from_proj),
        out_shape=jax.ShapeDtypeStruct((nb * nblk * tq, SW_Q_W), BF16),
        grid=(nb, nblk, SW_KV_HEADS // SW_PAIR),
        in_specs=[smem, smem,
                  pl.BlockSpec((tq, qw), lambda b, n, p: (cur_row(b, n, p), qb + p)),
                  prev_k,
                  pl.BlockSpec((tq, kw), lambda b, n, p: (cur_row(b, n, p), kb + p)),
                  prev_v,
                  pl.BlockSpec((tq, kw), lambda b, n, p: (cur_row(b, n, p), vb + p))],
        out_specs=pl.BlockSpec((tq, qw), lambda b, n, p: (b * nblk + n, p)),
        compiler_params=_cparams(("parallel", "parallel", "parallel")),
        name="swa",
    )(slopes, sinks, proj, k_prev_src, proj, v_prev_src, proj)


def _prep_in_proj(w_in):
    w_b = w_in.astype(BF16)
    sw_part = w_b[:, :, ORIG_SQ:]
    side = jnp.zeros(w_in.shape[:2] + (2 * LANES,), BF16)
    side = side.at[:, :, :DN_HEADS].set(w_b[:, :, ORIG_B:ORIG_A])
    side = side.at[:, :, LANES:LANES + DN_HEADS].set(w_b[:, :, ORIG_A:ORIG_SQ])
    return w_b, sw_part, side


def _ffn_half_step(x, xs, ssq, wg, wu, wd, layer, ln_next, name):
    hidden = _swiglu_up(xs, ssq, wg, wu, layer)
    return _residual_matmul(hidden, wd, layer, x, ln_next, 0.5, DOWN_TK, TN_HALF, name)


def _rows(a, r0, r1, c0, c1):
    return lax.slice(a, (r0, c0), (r1, c1))


def kernel(x_prompt, x_sample, state_dn_conv, state_dn_recurrent, cache_swa_k, cache_swa_v, ln_ffn1, w_ffn1_gate, w_ffn1_up, w_ffn1_down, ln_mix, w_in, dn_conv_w, dn_a_log, dn_dt_bias, dn_norm_w, swa_sinks, w_branch_a, w_branch_b, w_out, ln_ffn2, w_ffn2_gate, w_ffn2_up, w_ffn2_down, ln_final):
    slopes = 2.0 ** (-8.0 * jnp.arange(1, SW_HEADS + 1, dtype=F32) / SW_HEADS)
    x_dec = jnp.pad(x_sample, ((0, 0), (0, DEC_PAD - DEC_SEQ), (0, 0)))
    x = jnp.concatenate([x_prompt.reshape(M_PROMPT, D_MODEL), x_dec.reshape(M_SAMPLE, D_MODEL)], axis=0)
    zero_conv = jnp.zeros((BATCH, SUBLANES, DN_CONV_DIM), F32)
    zero_state = jnp.zeros((BATCH, DN_HEADS, DN_DK, DN_DV), F32)

    wg1, wu1, wd1 = w_ffn1_gate, w_ffn1_up, w_ffn1_down.astype(BF16)
    wg2, wu2, wd2 = w_ffn2_gate, w_ffn2_up, w_ffn2_down.astype(BF16)
    w_inb, w_sw, w_side = _prep_in_proj(w_in)
    w_a = w_branch_a.astype(BF16)
    w_b = w_branch_b.astype(BF16)
    w_o = w_out.astype(BF16)
    buf = cache_swa_k.shape[2]
    keep = min(WINDOW, SEQ)

    outs = [[] for _ in range(8)]
    xs, ssq = _prenorm(x, ln_ffn1[0])
    for l in range(DEPTH):
        x, xs, ssq = _ffn_half_step(x, xs, ssq, wg1, wu1, wd1, l, ln_mix[l], "ffn1_down")

        proj_dn = _normed_matmul(xs, ssq, w_inb, l, 0, DN_PART_W, F32, TN, "in_proj_dn")
        proj_sw = _normed_matmul(xs, ssq, w_sw, l, 0, SW_PART_W, F32, TN, "in_proj_sw")
        gates = _normed_matmul(xs, ssq, w_side, l, 0, 2 * LANES, F32, 2 * LANES, "in_proj_gates")

        conv_s = jnp.pad(state_dn_conv[l], ((0, 0), (SUBLANES - (DN_CONV - 1), 0), (0, 0)))
        dn_args = (dn_conv_w[l], dn_a_log[l], dn_dt_bias[l], dn_norm_w[l])
        oa_p, rec_p = _deltanet(proj_dn, gates, zero_conv, zero_state, *dn_args,
                                nb=BATCH, t=SEQ, c=DN_CHUNK, t_valid=DN_CHUNK, row0=0)
        oa_s, rec_s = _deltanet(proj_dn, gates, conv_s, state_dn_recurrent[l].astype(F32), *dn_args,
                                nb=DEC_BATCH, t=DEC_PAD, c=DEC_PAD, t_valid=DEC_SEQ, row0=M_PROMPT)

        ck = cache_swa_k[l].reshape(DEC_BATCH * buf, SW_KV_W)
        cv = cache_swa_v[l].reshape(DEC_BATCH * buf, SW_KV_W)
        sinks = swa_sinks[l].astype(F32)
        ob_p = _swa(proj_sw, proj_sw, proj_sw, slopes, sinks, nb=BATCH, nblk=SEQ // WINDOW,
                    tq=WINDOW, row0=0, prev_from_proj=True)
        ob_s = _swa(proj_sw, ck, cv, slopes, sinks, nb=DEC_BATCH, nblk=1,
                    tq=DEC_PAD, row0=M_PROMPT, prev_from_proj=False)

        o_a = jnp.concatenate([oa_p, oa_s], axis=0)
        o_b = jnp.concatenate([ob_p, ob_s], axis=0)
        merged = _branch_merge(o_a, o_b, w_a, w_b, l, proj_sw)
        x, xs, ssq = _residual_matmul(merged, w_o, l, x, ln_ffn2[l], 1.0, OUT_TK, TN, "out_proj")

        ln_next = ln_ffn1[l + 1] if l + 1 < DEPTH else ln_final
        x, xs, ssq = _ffn_half_step(x, xs, ssq, wg2, wu2, wd2, l, ln_next, "ffn2_down")

        last = [(b + 1) * SEQ for b in range(BATCH)]
        sample = lambda p, c0, c1: _rows(p, M_PROMPT, M_ROWS, c0, c1).reshape(DEC_BATCH, DEC_PAD, c1 - c0)[:, :DEC_SEQ]
        outs[0].append(jnp.stack([_rows(proj_dn, e - (DN_CONV - 1), e, OFF_QKV, OFF_QKV + DN_CONV_DIM) for e in last]))
        outs[1].append(rec_p)
        outs[2].append(jnp.stack([_rows(proj_sw, e - keep, e, OFF_SK, OFF_SK + SW_KV_W) for e in last])
                       .reshape(BATCH, keep, SW_KV_HEADS, SW_HD))
        outs[3].append(jnp.stack([_rows(proj_sw, e - keep, e, OFF_SV, OFF_SV + SW_KV_W) for e in last])
                       .reshape(BATCH, keep, SW_KV_HEADS, SW_HD))
        outs[4].append(sample(proj_dn, OFF_QKV, OFF_QKV + DN_CONV_DIM)[:, DEC_SEQ - (DN_CONV - 1):])
        outs[5].append(rec_s)
        new_k = sample(proj_sw, OFF_SK, OFF_SK + SW_KV_W).reshape(DEC_BATCH, DEC_SEQ, SW_KV_HEADS, SW_HD)
        new_v = sample(proj_sw, OFF_SV, OFF_SV + SW_KV_W).reshape(DEC_BATCH, DEC_SEQ, SW_KV_HEADS, SW_HD)
        outs[6].append(jnp.concatenate([cache_swa_k[l], new_k], axis=1)[:, -buf:])
        outs[7].append(jnp.concatenate([cache_swa_v[l], new_v], axis=1)[:, -buf:])

    y_prompt = _rmsnorm(x, ln_final, F32, 0, M_PROMPT).reshape(BATCH, SEQ, D_MODEL)
    y_sample = _rmsnorm(x, ln_final, F32, M_PROMPT, M_SAMPLE).reshape(DEC_BATCH, DEC_PAD, D_MODEL)[:, :DEC_SEQ]
    rec_dtype = state_dn_recurrent.dtype
    return (y_prompt, y_sample,
            jnp.stack(outs[0]), jnp.stack(outs[1]).astype(rec_dtype), jnp.stack(outs[2]), jnp.stack(outs[3]),
            jnp.stack(outs[4]), jnp.stack(outs[5]).astype(rec_dtype), jnp.stack(outs[6]), jnp.stack(outs[7]))
```

```python
import functools
import math

import jax
import jax.numpy as jnp
from jax import lax
from jax.experimental import pallas as pl
from jax.experimental.pallas import tpu as pltpu

F32 = jnp.float32
BF16 = jnp.bfloat16
HIGHEST = lax.Precision.HIGHEST

D_MODEL = 4096
BATCH = 4
SEQ = 2048
DEPTH = 2
DEC_BATCH = 32
DEC_SEQ = 4
DN_HEADS = 16
DN_DK = 128
DN_DV = 128
DN_CONV = 4
DN_CHUNK = 64
DN_QK_W = DN_HEADS * DN_DK
DN_V_W = DN_HEADS * DN_DV
DN_CONV_DIM = 2 * DN_QK_W + DN_V_W
SW_HEADS = 32
SW_KV_HEADS = 8
SW_GROUP = SW_HEADS // SW_KV_HEADS
SW_HD = 64
SW_Q_W = SW_HEADS * SW_HD
SW_KV_W = SW_KV_HEADS * SW_HD
WINDOW = 128
D_FF = 11008
EPS = 1e-6

SUBLANES = 8
LANES = 128
VMEM_LIMIT = 56 * 1024 * 1024

DEC_PAD = SUBLANES
M_PROMPT = BATCH * SEQ
M_SAMPLE = DEC_BATCH * DEC_SEQ
M_ROWS = M_PROMPT + M_SAMPLE
TM = M_ROWS // 8
TR_ALL = M_ROWS // 26
TR_PROMPT = 256
TR_SAMPLE = M_SAMPLE
TN = 1024
TN_HALF = 512

OFF_QKV = 0
OFF_Z = OFF_QKV + DN_CONV_DIM
DN_PART_W = OFF_Z + DN_V_W
OFF_SQ = 0
OFF_SK = OFF_SQ + SW_Q_W
OFF_SV = OFF_SK + SW_KV_W
OFF_GA = OFF_SV + SW_KV_W
OFF_GB = OFF_GA + D_MODEL
SW_PART_W = OFF_GB + D_MODEL
ORIG_B = DN_PART_W
ORIG_A = ORIG_B + DN_HEADS
ORIG_SQ = ORIG_A + DN_HEADS

FF_TN = 256
DOWN_TK = D_FF // 2
OUT_TK = D_MODEL // 2

DN_STACK = 2
DN_GROUP_BATCH = 8
SW_PAIR = 2


def _cparams(sem):
    return pltpu.CompilerParams(dimension_semantics=sem, vmem_limit_bytes=VMEM_LIMIT)


def _row_rsqrt(ssq_ref, d):
    return lax.rsqrt(ssq_ref[:, 0:1] * (1.0 / d) + EPS)


def _prenorm_kernel(x_ref, w_ref, xs_ref, ssq_ref):
    x = x_ref[...]
    xs_ref[...] = (x * w_ref[...]).astype(xs_ref.dtype)
    ssq_ref[...] = jnp.broadcast_to(jnp.sum(x * x, -1, keepdims=True), ssq_ref.shape)


def _prenorm(x, w):
    m, d = x.shape
    tr = TR_ALL
    return pl.pallas_call(
        _prenorm_kernel,
        out_shape=(jax.ShapeDtypeStruct((m, d), BF16), jax.ShapeDtypeStruct((m, LANES), F32)),
        grid=(m // tr,),
        in_specs=[pl.BlockSpec((tr, d), lambda i: (i, 0)), pl.BlockSpec((1, d), lambda i: (0, 0))],
        out_specs=(pl.BlockSpec((tr, d), lambda i: (i, 0)), pl.BlockSpec((tr, LANES), lambda i: (i, 0))),
        compiler_params=_cparams(("parallel",)),
        name="prenorm",
    )(x, w.reshape(1, d))


def _rmsnorm_kernel(x_ref, w_ref, o_ref):
    x = x_ref[...]
    y = x * lax.rsqrt(jnp.mean(x * x, -1, keepdims=True) + EPS)
    o_ref[...] = (y * w_ref[...]).astype(o_ref.dtype)


def _rmsnorm(x, w, out_dtype, row0, nrows, tr):
    d = x.shape[1]
    blk0 = row0 // tr
    return pl.pallas_call(
        _rmsnorm_kernel,
        out_shape=jax.ShapeDtypeStruct((nrows, d), out_dtype),
        grid=(nrows // tr,),
        in_specs=[pl.BlockSpec((tr, d), lambda i: (blk0 + i, 0)), pl.BlockSpec((1, d), lambda i: (0, 0))],
        out_specs=pl.BlockSpec((tr, d), lambda i: (i, 0)),
        compiler_params=_cparams(("parallel",)),
        name="rmsnorm",
    )(x, w.reshape(1, d))


def _mm_kernel(a_ref, ssq_ref, w_ref, o_ref):
    d = jnp.dot(a_ref[...], w_ref[...], preferred_element_type=F32)
    o_ref[...] = (d * _row_rsqrt(ssq_ref, a_ref.shape[1])).astype(o_ref.dtype)


def _normed_matmul(xs, ssq, w, layer, col0, ncols, out_dtype, tn, name):
    m, k = xs.shape
    blk0 = col0 // tn
    return pl.pallas_call(
        _mm_kernel,
        out_shape=jax.ShapeDtypeStruct((m, ncols), out_dtype),
        grid=(m // TM, ncols // tn),
        in_specs=[pl.BlockSpec((TM, k), lambda i, j: (i, 0)),
                  pl.BlockSpec((TM, LANES), lambda i, j: (i, 0)),
                  pl.BlockSpec((None, k, tn), lambda i, j: (layer, 0, blk0 + j))],
        out_specs=pl.BlockSpec((TM, tn), lambda i, j: (i, j)),
        compiler_params=_cparams(("parallel", "parallel")),
        name=name,
    )(xs, ssq, w)


def _swiglu_kernel(a_ref, ssq_ref, wg_ref, wu_ref, o_ref, wgb_ref, wub_ref):
    @pl.when(pl.program_id(1) == 0)
    def _():
        wgb_ref[...] = wg_ref[...].astype(BF16)
        wub_ref[...] = wu_ref[...].astype(BF16)

    a = a_ref[...]
    r = _row_rsqrt(ssq_ref, a_ref.shape[1])
    g = jnp.dot(a, wgb_ref[...], preferred_element_type=F32) * r
    u = jnp.dot(a, wub_ref[...], preferred_element_type=F32) * r
    o_ref[...] = (g * jax.nn.sigmoid(g) * u).astype(o_ref.dtype)


def _swiglu_up(xs, ssq, wg, wu, layer):
    m, k = xs.shape
    n = wg.shape[2]
    tn = FF_TN
    wspec = pl.BlockSpec((None, k, tn), lambda j, i: (layer, 0, j))
    return pl.pallas_call(
        _swiglu_kernel,
        out_shape=jax.ShapeDtypeStruct((m, n), BF16),
        grid=(n // tn, m // TM),
        in_specs=[pl.BlockSpec((TM, k), lambda j, i: (i, 0)),
                  pl.BlockSpec((TM, LANES), lambda j, i: (i, 0)), wspec, wspec],
        out_specs=pl.BlockSpec((TM, tn), lambda j, i: (i, j)),
        scratch_shapes=[pltpu.VMEM((k, tn), BF16), pltpu.VMEM((k, tn), BF16)],
        compiler_params=_cparams(("parallel", "arbitrary")),
        name="ffn_gate_up",
    )(xs, ssq, wg, wu)


def _residual_mm_kernel(a_ref, w_ref, r_ref, ln_ref, o_ref, xs_ref, ssq_ref, *, scale, nk):
    j = pl.program_id(1)
    kk = pl.program_id(2)
    d = jnp.dot(a_ref[...], w_ref[...], preferred_element_type=F32)
    if scale != 1.0:
        d = d * scale

    def finish(val):
        o_ref[...] = val
        xs_ref[...] = (val * ln_ref[...]).astype(xs_ref.dtype)
        part = jnp.broadcast_to(jnp.sum(val * val, -1, keepdims=True), ssq_ref.shape)

        @pl.when(j == 0)
        def _():
            ssq_ref[...] = part

        @pl.when(j != 0)
        def _():
            ssq_ref[...] += part

    if nk == 1:
        finish(r_ref[...] + d)
    else:
        @pl.when(kk == 0)
        def _():
            o_ref[...] = r_ref[...] + d

        @pl.when((kk > 0) & (kk < nk - 1))
        def _():
            o_ref[...] += d

        @pl.when(kk == nk - 1)
        def _():
            finish(o_ref[...] + d)


def _residual_matmul(a, w, layer, res, ln_next, scale, tk, tn, name):
    m, k = a.shape
    n = w.shape[2]
    nk = k // tk
    return pl.pallas_call(
        functools.partial(_residual_mm_kernel, scale=scale, nk=nk),
        out_shape=(jax.ShapeDtypeStruct((m, n), F32), jax.ShapeDtypeStruct((m, n), BF16),
                   jax.ShapeDtypeStruct((m, LANES), F32)),
        grid=(m // TM, n // tn, nk),
        in_specs=[pl.BlockSpec((TM, tk), lambda i, j, kk: (i, kk)),
                  pl.BlockSpec((None, tk, tn), lambda i, j, kk: (layer, kk, j)),
                  pl.BlockSpec((TM, tn), lambda i, j, kk: (i, j)),
                  pl.BlockSpec((1, tn), lambda i, j, kk: (0, j))],
        out_specs=(pl.BlockSpec((TM, tn), lambda i, j, kk: (i, j)),
                   pl.BlockSpec((TM, tn), lambda i, j, kk: (i, j)),
                   pl.BlockSpec((TM, LANES), lambda i, j, kk: (i, 0))),
        compiler_params=_cparams(("parallel", "arbitrary", "arbitrary")),
        name=name,
    )(a, w, res, ln_next.reshape(1, n))


def _merge_kernel(oa_ref, ob_ref, wa_ref, wb_ref, ga_ref, gb_ref, o_ref):
    ya = jnp.dot(oa_ref[...], wa_ref[...], preferred_element_type=F32)
    yb = jnp.dot(ob_ref[...], wb_ref[...], preferred_element_type=F32)
    o = jax.nn.sigmoid(ga_ref[...]) * ya + jax.nn.sigmoid(gb_ref[...]) * yb
    o_ref[...] = o.astype(o_ref.dtype)


def _branch_merge(o_a, o_b, w_a, w_b, layer, proj):
    m, ka = o_a.shape
    kb = o_b.shape[1]
    n = w_a.shape[2]
    tn = TN_HALF
    ga_blk = OFF_GA // tn
    gb_blk = OFF_GB // tn
    return pl.pallas_call(
        _merge_kernel,
        out_shape=jax.ShapeDtypeStruct((m, n), BF16),
        grid=(m // TM, n // tn),
        in_specs=[pl.BlockSpec((TM, ka), lambda i, j: (i, 0)),
                  pl.BlockSpec((TM, kb), lambda i, j: (i, 0)),
                  pl.BlockSpec((None, ka, tn), lambda i, j: (layer, 0, j)),
                  pl.BlockSpec((None, kb, tn), lambda i, j: (layer, 0, j)),
                  pl.BlockSpec((TM, tn), lambda i, j: (i, ga_blk + j)),
                  pl.BlockSpec((TM, tn), lambda i, j: (i, gb_blk + j))],
        out_specs=pl.BlockSpec((TM, tn), lambda i, j: (i, j)),
        compiler_params=_cparams(("parallel", "parallel")),
        name="branch_merge",
    )(o_a, o_b, w_a, w_b, proj, proj)


def _dot_nt(a, b, precision=None):
    return lax.dot_general(a, b, (((1,), (1,)), ((), ())), precision=precision,
                           preferred_element_type=F32)


def _dot_tn(a, b):
    return lax.dot_general(a, b, (((0,), (0,)), ((), ())), preferred_element_type=F32)


def _dot_hi(a, b):
    return jnp.dot(a, b, precision=HIGHEST, preferred_element_type=F32)


def _split(x):
    hi = x.astype(BF16)
    return hi, (x - hi.astype(F32)).astype(BF16)


def _bdot(a, b):
    return jnp.einsum("gij,gjk->gik", a.astype(BF16), b.astype(BF16), preferred_element_type=F32)


def _bdot_nt(a, b):
    return jnp.einsum("gik,gjk->gij", a.astype(BF16), b.astype(BF16), preferred_element_type=F32)


def _bdot_3x(a, b):
    return _bdot(a[0], b[0]) + _bdot(a[0], b[1]) + _bdot(a[1], b[0])


def _unit_lower_inverse_minus_eye(a, n, c):
    ri = lax.broadcasted_iota(jnp.int32, (n, n), 0)
    ci = lax.broadcasted_iota(jnp.int32, (n, n), 1)
    base = min(16, c)
    p = jnp.where((ri // base) == (ci // base), a, 0.0)
    nt = -p
    for _ in range(int(math.log2(base)) - 1):
        p = _bdot(p, p)
        nt = nt + p + _bdot(nt, p)
    blk = base
    while blk < c:
        pair = (ri // (2 * blk)) == (ci // (2 * blk))
        lower = pair & (((ri // blk) % 2) == 1) & (((ci // blk) % 2) == 0)
        l = jnp.where(lower, a, 0.0)
        lt = l + _bdot(l, nt)
        nt = nt - lt - _bdot(nt, lt)
        blk *= 2
    return nt


def _dn_kernel(qc_ref, kc_ref, vc_ref, qp_ref, kp_ref, vp_ref, qi_ref, ki_ref, vi_ref,
               z_ref, b_ref, a_ref, wq_ref, wk_ref, wv_ref, alog_ref, dtb_ref, nw_ref, s0_ref,
               o_ref, sout_ref, s_ref, ext_ref, *, c, t_valid):
    ch = pl.program_id(1)
    nch = pl.num_programs(1)
    n = DN_STACK * c

    @pl.when(ch == 0)
    def _():
        s_ref[...] = s0_ref[0]

    def conv(cur_ref, prev_ref, init_ref, w_ref):
        prev = jnp.where(ch == 0, init_ref[0], prev_ref[...])
        ext_ref[0:SUBLANES, :] = prev
        ext_ref[SUBLANES:SUBLANES + c, :] = cur_ref[...]
        first = SUBLANES - (DN_CONV - 1)
        y = ext_ref[first:first + c, :] * w_ref[0:1, :]
        for j in range(1, DN_CONV):
            y = y + ext_ref[first + j:first + j + c, :] * w_ref[j:j + 1, :]
        return y * jax.nn.sigmoid(y)

    yq = conv(qc_ref, qp_ref, qi_ref, wq_ref)
    yk = conv(kc_ref, kp_ref, ki_ref, wk_ref)
    yv = conv(vc_ref, vp_ref, vi_ref, wv_ref)
    zz = z_ref[...]

    beta_all = jax.nn.sigmoid(b_ref[...])
    araw = a_ref[...] + dtb_ref[...]
    softplus = jnp.maximum(araw, 0.0) + jnp.log1p(jnp.exp(-jnp.abs(araw)))
    g_all = -jnp.exp(alog_ref[...]) * softplus
    if t_valid < c:
        row_ok = lax.broadcasted_iota(jnp.int32, (c, LANES), 0) < t_valid
        beta_all = jnp.where(row_ok, beta_all, 0.0)
        g_all = jnp.where(row_ok, g_all, 0.0)
    tri = lax.broadcasted_iota(jnp.int32, (c, c), 0) >= lax.broadcasted_iota(jnp.int32, (c, c), 1)
    g_cum = _dot_hi(tri.astype(F32), g_all)
    lri = lax.broadcasted_iota(jnp.int32, (LANES, LANES), 0)
    lci = lax.broadcasted_iota(jnp.int32, (LANES, LANES), 1)
    g_cum_t = _dot_nt((lri == lci).astype(F32), g_cum, HIGHEST)

    ri = lax.broadcasted_iota(jnp.int32, (n, n), 0)
    ci = lax.broadcasted_iota(jnp.int32, (n, n), 1)
    same_head = (ri // c) == (ci // c)
    causal = same_head & ((ri % c) >= (ci % c))
    strict = same_head & ((ri % c) > (ci % c))

    for g0 in range(0, DN_HEADS // DN_STACK, DN_GROUP_BATCH):
        groups = [[(g0 + gi) * DN_STACK + j for j in range(DN_STACK)] for gi in range(DN_GROUP_BATCH)]

        def stack(y):
            return jnp.stack([jnp.concatenate([y[:, h * DN_DK:(h + 1) * DN_DK] for h in hs], axis=0) for hs in groups])

        def column(m):
            return jnp.stack([jnp.concatenate([m[:, h:h + 1] for h in hs], axis=0) for hs in groups])

        q = stack(yq)
        k = stack(yk)
        v = stack(yv)
        bcol = column(beta_all)
        gcol = column(g_cum)
        grow = jnp.stack([jnp.concatenate([g_cum_t[h:h + 1, :] for h in hs], axis=1) for hs in groups])
        glast = jnp.stack([jnp.concatenate([jnp.broadcast_to(g_cum[c - 1:c, h:h + 1], (c, 1)) for h in hs], axis=0)
                           for hs in groups])
        q = q * lax.rsqrt(jnp.sum(q * q, -1, keepdims=True) + 1e-6) * (DN_DK ** -0.5)
        k = k * lax.rsqrt(jnp.sum(k * k, -1, keepdims=True) + 1e-6)
        decay = jnp.exp(jnp.where(causal, gcol - grow, -jnp.inf))
        kb = k * bcol
        a_mat = jnp.where(strict, _bdot_nt(kb, k) * decay, 0.0)
        t_low = _unit_lower_inverse_minus_eye(a_mat, n, c)
        eg = jnp.exp(gcol)
        rhs = jnp.concatenate([v * bcol, kb * eg], axis=2)
        sol = rhs + _bdot_3x(_split(t_low), _split(rhs))
        u_base = sol[:, :, :DN_DV]
        w_cum = sol[:, :, DN_DV:].astype(BF16)
        a_qk = _bdot_nt(q, k) * decay
        q_dec = (q * eg).astype(BF16)
        k_dec = (k * jnp.exp(glast - gcol)).astype(BF16)

        s_old = [[s_ref[h] for h in hs] for hs in groups]
        ws, qs = [], []
        for gi, hs in enumerate(groups):
            wg, qg = [], []
            for j in range(DN_STACK):
                rows = slice(j * c, (j + 1) * c)
                lhs = jnp.concatenate([w_cum[gi, rows], q_dec[gi, rows]], axis=0)
                r = jnp.dot(lhs, s_old[gi][j].astype(BF16), preferred_element_type=F32)
                wg.append(r[:c])
                qg.append(r[c:])
            ws.append(jnp.concatenate(wg, axis=0))
            qs.append(jnp.concatenate(qg, axis=0))
        u = u_base - jnp.stack(ws)
        u_b = u.astype(BF16)
        o = jnp.stack(qs) + _bdot(a_qk, u_b)
        for gi, hs in enumerate(groups):
            for j, h in enumerate(hs):
                rows = slice(j * c, (j + 1) * c)
                g_tot = jnp.exp(g_cum[c - 1:c, h:h + 1])
                s_ref[h] = s_old[gi][j] * g_tot + _dot_tn(k_dec[gi, rows], u_b[gi, rows])
        o = o * lax.rsqrt(jnp.mean(o * o, -1, keepdims=True) + EPS) * nw_ref[...]
        z = stack(zz)
        o = o * (z * jax.nn.sigmoid(z))
        for gi, hs in enumerate(groups):
            for j, h in enumerate(hs):
                o_ref[:, h * DN_DV:(h + 1) * DN_DV] = o[gi, j * c:(j + 1) * c].astype(o_ref.dtype)

    @pl.when(ch == nch - 1)
    def _():
        sout_ref[0] = s_ref[...]


def _deltanet(proj, gates, conv_init, s0, conv_w, a_log, dt_bias, norm_w, *, nb, t, c, t_valid, row0):
    w = DN_QK_W
    nch = t // c
    cur_row = lambda b, k: row0 // c + b * nch + k
    prev_row = lambda b, k: jnp.maximum(row0 // SUBLANES + (b * t + k * c) // SUBLANES - 1, 0)
    kq, kk, kv, kz = (OFF_QKV // w, (OFF_QKV + DN_QK_W) // w, (OFF_QKV + 2 * DN_QK_W) // w, OFF_Z // w)

    def cur(col):
        return pl.BlockSpec((c, w), lambda b, k: (cur_row(b, k), col))

    def prev(col):
        return pl.BlockSpec((SUBLANES, w), lambda b, k: (prev_row(b, k), col))

    def init(col):
        return pl.BlockSpec((1, SUBLANES, w), lambda b, k: (b, 0, col))

    def taps(col):
        return pl.BlockSpec((DN_CONV, w), lambda b, k: (0, col))

    row = lambda: pl.BlockSpec((1, LANES), lambda b, k: (0, 0))
    pad = lambda x: jnp.zeros((1, LANES), F32).at[0, :DN_HEADS].set(x.astype(F32))
    state = pl.BlockSpec((1, DN_HEADS, DN_DK, DN_DV), lambda b, k: (b, 0, 0, 0))
    o, s_new = pl.pallas_call(
        functools.partial(_dn_kernel, c=c, t_valid=t_valid),
        out_shape=(jax.ShapeDtypeStruct((nb * t, DN_V_W), BF16),
                   jax.ShapeDtypeStruct((nb, DN_HEADS, DN_DK, DN_DV), F32)),
        grid=(nb, nch),
        in_specs=[cur(kq), cur(kk), cur(kv), prev(kq), prev(kk), prev(kv),
                  init(kq), init(kk), init(kv), cur(kz),
                  pl.BlockSpec((c, LANES), lambda b, k: (cur_row(b, k), 0)),
                  pl.BlockSpec((c, LANES), lambda b, k: (cur_row(b, k), 1)),
                  taps(kq), taps(kk), taps(kv), row(), row(), row(), state],
        out_specs=(pl.BlockSpec((c, w), lambda b, k: (b * nch + k, 0)), state),
        scratch_shapes=[pltpu.VMEM((DN_HEADS, DN_DK, DN_DV), F32),
                        pltpu.VMEM((c + SUBLANES, w), F32)],
        compiler_params=_cparams(("parallel", "arbitrary")),
        name="deltanet",
    )(proj, proj, proj, proj, proj, proj, conv_init, conv_init, conv_init, proj, gates, gates,
      conv_w, conv_w, conv_w, pad(a_log), pad(dt_bias), norm_w.reshape(1, DN_DV).astype(F32), s0)
    return o, s_new


def _swa_kernel(slopes_ref, sinks_ref, q_ref, kp_ref, kc_ref, vp_ref, vc_ref, o_ref, *, tq, first_prev_valid):
    blk = pl.program_id(1)
    pair = pl.program_id(2)
    tk = WINDOW + tq
    n = SW_GROUP * tq
    k = jnp.concatenate([kp_ref[...], kc_ref[...]], axis=0)
    v = jnp.concatenate([vp_ref[...], vc_ref[...]], axis=0)
    k_swap = pltpu.roll(k, SW_HD, 1)
    v_swap = pltpu.roll(v, SW_HD, 1)
    low_kv = lax.broadcasted_iota(jnp.int32, (tk, LANES), 1) < SW_HD
    q = q_ref[...]
    lane_low = lax.broadcasted_iota(jnp.int32, (n, LANES), 1) < SW_HD
    grp = lax.broadcasted_iota(jnp.int32, (n, 1), 0) // tq
    own_half = lane_low == ((grp % 2) == 0)
    low_out = lax.broadcasted_iota(jnp.int32, (tq, LANES), 1) < SW_HD
    ri = lax.broadcasted_iota(jnp.int32, (n, tk), 0)
    ci = lax.broadcasted_iota(jnp.int32, (n, tk), 1)
    dist = (ri % tq) + WINDOW - ci
    valid = (dist >= 0) & (dist <= WINDOW)
    if not first_prev_valid:
        valid = valid & ((ci >= WINDOW) | (blk > 0))
    neg_dist = jnp.where(valid, -dist.astype(F32), -jnp.inf)
    for kh in range(SW_PAIR):
        head0 = (pair * SW_PAIR + kh) * SW_GROUP
        slope = jnp.zeros((n, 1), F32)
        sink = jnp.zeros((n, 1), F32)
        for g in range(SW_GROUP):
            slope = jnp.where(grp == g, slopes_ref[head0 + g], slope)
            sink = jnp.where(grp == g, sinks_ref[head0 + g], sink)
        kk = (jnp.where(low_kv, k, k_swap) if kh == 0 else jnp.where(low_kv, k_swap, k)).astype(BF16)
        vv = (jnp.where(low_kv, v, v_swap) if kh == 0 else jnp.where(low_kv, v_swap, v)).astype(BF16)
        blocks = [kh * (SW_GROUP // 2) + g // 2 for g in range(SW_GROUP)]
        qs = jnp.concatenate([q[:, b * LANES:(b + 1) * LANES] for b in blocks], axis=0)
        qm = jnp.where(own_half, qs * (SW_HD ** -0.5), 0.0).astype(BF16)
        s = _dot_nt(qm, kk) + slope * neg_dist
        m = jnp.maximum(jnp.max(s, -1, keepdims=True), sink)
        p = jnp.exp(s - m)
        p = p / (jnp.sum(p, -1, keepdims=True) + jnp.exp(sink - m))
        o = jnp.dot(p.astype(BF16), vv, preferred_element_type=F32)
        for j in range(SW_GROUP // 2):
            even = o[(2 * j) * tq:(2 * j + 1) * tq]
            odd = o[(2 * j + 1) * tq:(2 * j + 2) * tq]
            b = kh * (SW_GROUP // 2) + j
            o_ref[:, b * LANES:(b + 1) * LANES] = jnp.where(low_out, even, odd).astype(o_ref.dtype)


def _swa(proj, k_prev_src, v_prev_src, slopes, sinks, *, nb, nblk, tq, row0, prev_from_proj):
    qw = SW_PAIR * SW_GROUP * SW_HD
    kw = SW_PAIR * SW_HD
    qb, kb, vb = OFF_SQ // qw, OFF_SK // kw, OFF_SV // kw
    cur_row = lambda b, n, p: row0 // tq + b * nblk + n
    if prev_from_proj:
        prev_k = pl.BlockSpec((WINDOW, kw), lambda b, n, p: (b * nblk + jnp.maximum(n - 1, 0), kb + p))
        prev_v = pl.BlockSpec((WINDOW, kw), lambda b, n, p: (b * nblk + jnp.maximum(n - 1, 0), vb + p))
    else:
        prev_k = pl.BlockSpec((WINDOW, kw), lambda b, n, p: (b, p))
        prev_v = pl.BlockSpec((WINDOW, kw), lambda b, n, p: (b, p))
    smem = pl.BlockSpec(memory_space=pltpu.SMEM)
    return pl.pallas_call(
        functools.partial(_swa_kernel, tq=tq, first_prev_valid=not prev_from_proj),
        out_shape=jax.ShapeDtypeStruct((nb * nblk * tq, SW_Q_W), BF16),
        grid=(nb, nblk, SW_KV_HEADS // SW_PAIR),
        in_specs=[smem, smem,
                  pl.BlockSpec((tq, qw), lambda b, n, p: (cur_row(b, n, p), qb + p)),
                  prev_k,
                  pl.BlockSpec((tq, kw), lambda b, n, p: (cur_row(b, n, p), kb + p)),
                  prev_v,
                  pl.BlockSpec((tq, kw), lambda b, n, p: (cur_row(b, n, p), vb + p))],
        out_specs=pl.BlockSpec((tq, qw), lambda b, n, p: (b * nblk + n, p)),
        compiler_params=_cparams(("parallel", "parallel", "parallel")),
        name="swa",
    )(slopes, sinks, proj, k_prev_src, proj, v_prev_src, proj)


def _prep_in_proj(w_in):
    w_b = w_in.astype(BF16)
    sw_part = w_b[:, :, ORIG_SQ:]
    side = jnp.zeros(w_in.shape[:2] + (2 * LANES,), BF16)
    side = side.at[:, :, :DN_HEADS].set(w_b[:, :, ORIG_B:ORIG_A])
    side = side.at[:, :, LANES:LANES + DN_HEADS].set(w_b[:, :, ORIG_A:ORIG_SQ])
    return w_b, sw_part, side


def _ffn_half_step(x, xs, ssq, wg, wu, wd, layer, ln_next, name):
    hidden = _swiglu_up(xs, ssq, wg, wu, layer)
    return _residual_matmul(hidden, wd, layer, x, ln_next, 0.5, DOWN_TK, TN_HALF, name)


def _rows(a, r0, r1, c0, c1):
    return lax.slice(a, (r0, c0), (r1, c1))


def _pad_sample(p):
    w = p.shape[1]
    s = _rows(p, M_PROMPT, M_ROWS, 0, w).reshape(DEC_BATCH, DEC_SEQ, w)
    return jnp.pad(s, ((0, 0), (0, DEC_PAD - DEC_SEQ), (0, 0))).reshape(DEC_BATCH * DEC_PAD, w)


def _unpad_sample(o):
    w = o.shape[1]
    return o.reshape(DEC_BATCH, DEC_PAD, w)[:, :DEC_SEQ].reshape(M_SAMPLE, w)


def kernel(x_prompt, x_sample, state_dn_conv, state_dn_recurrent, cache_swa_k, cache_swa_v, ln_ffn1, w_ffn1_gate, w_ffn1_up, w_ffn1_down, ln_mix, w_in, dn_conv_w, dn_a_log, dn_dt_bias, dn_norm_w, swa_sinks, w_branch_a, w_branch_b, w_out, ln_ffn2, w_ffn2_gate, w_ffn2_up, w_ffn2_down, ln_final):
    slopes = 2.0 ** (-8.0 * jnp.arange(1, SW_HEADS + 1, dtype=F32) / SW_HEADS)
    x = jnp.concatenate([x_prompt.reshape(M_PROMPT, D_MODEL), x_sample.reshape(M_SAMPLE, D_MODEL)], axis=0)
    zero_conv = jnp.zeros((BATCH, SUBLANES, DN_CONV_DIM), F32)
    zero_state = jnp.zeros((BATCH, DN_HEADS, DN_DK, DN_DV), F32)

    wg1, wu1, wd1 = w_ffn1_gate, w_ffn1_up, w_ffn1_down.astype(BF16)
    wg2, wu2, wd2 = w_ffn2_gate, w_ffn2_up, w_ffn2_down.astype(BF16)
    w_inb, w_sw, w_side = _prep_in_proj(w_in)
    w_a = w_branch_a.astype(BF16)
    w_b = w_branch_b.astype(BF16)
    w_o = w_out.astype(BF16)
    buf = cache_swa_k.shape[2]
    keep = min(WINDOW, SEQ)

    outs = [[] for _ in range(8)]
    xs, ssq = _prenorm(x, ln_ffn1[0])
    for l in range(DEPTH):
        x, xs, ssq = _ffn_half_step(x, xs, ssq, wg1, wu1, wd1, l, ln_mix[l], "ffn1_down")

        proj_dn = _normed_matmul(xs, ssq, w_inb, l, 0, DN_PART_W, F32, TN, "in_proj_dn")
        proj_sw = _normed_matmul(xs, ssq, w_sw, l, 0, SW_PART_W, F32, TN, "in_proj_sw")
        gates = _normed_matmul(xs, ssq, w_side, l, 0, 2 * LANES, F32, 2 * LANES, "in_proj_gates")

        conv_s = jnp.pad(state_dn_conv[l], ((0, 0), (SUBLANES - (DN_CONV - 1), 0), (0, 0)))
        dn_args = (dn_conv_w[l], dn_a_log[l], dn_dt_bias[l], dn_norm_w[l])
        oa_p, rec_p = _deltanet(proj_dn, gates, zero_conv, zero_state, *dn_args,
                                nb=BATCH, t=SEQ, c=DN_CHUNK, t_valid=DN_CHUNK, row0=0)
        oa_s, rec_s = _deltanet(_pad_sample(proj_dn), _pad_sample(gates), conv_s,
                                state_dn_recurrent[l].astype(F32), *dn_args,
                                nb=DEC_BATCH, t=DEC_PAD, c=DEC_PAD, t_valid=DEC_SEQ, row0=0)

        ck = cache_swa_k[l].reshape(DEC_BATCH * buf, SW_KV_W)
        cv = cache_swa_v[l].reshape(DEC_BATCH * buf, SW_KV_W)
        sinks = swa_sinks[l].astype(F32)
        ob_p = _swa(proj_sw, proj_sw, proj_sw, slopes, sinks, nb=BATCH, nblk=SEQ // WINDOW,
                    tq=WINDOW, row0=0, prev_from_proj=True)
        ob_s = _swa(_pad_sample(proj_sw), ck, cv, slopes, sinks, nb=DEC_BATCH, nblk=1,
                    tq=DEC_PAD, row0=0, prev_from_proj=False)

        o_a = jnp.concatenate([oa_p, _unpad_sample(oa_s)], axis=0)
        o_b = jnp.concatenate([ob_p, _unpad_sample(ob_s)], axis=0)
        merged = _branch_merge(o_a, o_b, w_a, w_b, l, proj_sw)
        x, xs, ssq = _residual_matmul(merged, w_o, l, x, ln_ffn2[l], 1.0, OUT_TK, TN, "out_proj")

        ln_next = ln_ffn1[l + 1] if l + 1 < DEPTH else ln_final
        x, xs, ssq = _ffn_half_step(x, xs, ssq, wg2, wu2, wd2, l, ln_next, "ffn2_down")

        last = [(b + 1) * SEQ for b in range(BATCH)]
        sample = lambda p, c0, c1: _rows(p, M_PROMPT, M_ROWS, c0, c1).reshape(DEC_BATCH, DEC_SEQ, c1 - c0)
        outs[0].append(jnp.stack([_rows(proj_dn, e - (DN_CONV - 1), e, OFF_QKV, OFF_QKV + DN_CONV_DIM) for e in last]))
        outs[1].append(rec_p)
        outs[2].append(jnp.stack([_rows(proj_sw, e - keep, e, OFF_SK, OFF_SK + SW_KV_W) for e in last])
                       .reshape(BATCH, keep, SW_KV_HEADS, SW_HD))
        outs[3].append(jnp.stack([_rows(proj_sw, e - keep, e, OFF_SV, OFF_SV + SW_KV_W) for e in last])
                       .reshape(BATCH, keep, SW_KV_HEADS, SW_HD))
        outs[4].append(sample(proj_dn, OFF_QKV, OFF_QKV + DN_CONV_DIM)[:, DEC_SEQ - (DN_CONV - 1):])
        outs[5].append(rec_s)
        new_k = sample(proj_sw, OFF_SK, OFF_SK + SW_KV_W).reshape(DEC_BATCH, DEC_SEQ, SW_KV_HEADS, SW_HD)
        new_v = sample(proj_sw, OFF_SV, OFF_SV + SW_KV_W).reshape(DEC_BATCH, DEC_SEQ, SW_KV_HEADS, SW_HD)
        outs[6].append(jnp.concatenate([cache_swa_k[l], new_k], axis=1)[:, -buf:])
        outs[7].append(jnp.concatenate([cache_swa_v[l], new_v], axis=1)[:, -buf:])

    y_prompt = _rmsnorm(x, ln_final, F32, 0, M_PROMPT, TR_PROMPT).reshape(BATCH, SEQ, D_MODEL)
    y_sample = _rmsnorm(x, ln_final, F32, M_PROMPT, M_SAMPLE, TR_SAMPLE).reshape(DEC_BATCH, DEC_SEQ, D_MODEL)
    rec_dtype = state_dn_recurrent.dtype
    return (y_prompt, y_sample,
            jnp.stack(outs[0]), jnp.stack(outs[1]).astype(rec_dtype), jnp.stack(outs[2]), jnp.stack(outs[3]),
            jnp.stack(outs[4]), jnp.stack(outs[5]).astype(rec_dtype), jnp.stack(outs[6]), jnp.stack(outs[7]))
```

```python
import functools
import math

import jax
import jax.numpy as jnp
from jax import lax
from jax.experimental import pallas as pl
from jax.experimental.pallas import tpu as pltpu

F32 = jnp.float32
BF16 = jnp.bfloat16
HIGHEST = lax.Precision.HIGHEST

D_MODEL = 4096
BATCH = 4
SEQ = 2048
DEPTH = 2
DEC_BATCH = 32
DEC_SEQ = 4
DN_HEADS = 16
DN_DK = 128
DN_DV = 128
DN_CONV = 4
DN_CHUNK = 64
DN_QK_W = DN_HEADS * DN_DK
DN_V_W = DN_HEADS * DN_DV
DN_CONV_DIM = 2 * DN_QK_W + DN_V_W
SW_HEADS = 32
SW_KV_HEADS = 8
SW_GROUP = SW_HEADS // SW_KV_HEADS
SW_HD = 64
SW_Q_W = SW_HEADS * SW_HD
SW_KV_W = SW_KV_HEADS * SW_HD
WINDOW = 128
D_FF = 11008
EPS = 1e-6

SUBLANES = 8
LANES = 128
VMEM_LIMIT = 56 * 1024 * 1024

DEC_PAD = SUBLANES
M_PROMPT = BATCH * SEQ
M_SAMPLE = DEC_BATCH * DEC_SEQ
M_ROWS = M_PROMPT + M_SAMPLE
TM = M_ROWS // 8
TR_ALL = M_ROWS // 26
TR_PROMPT = 256
TR_SAMPLE = M_SAMPLE
TN = 1024
TN_HALF = 512

OFF_QKV = 0
OFF_Z = OFF_QKV + DN_CONV_DIM
DN_PART_W = OFF_Z + DN_V_W
OFF_SQ = 0
OFF_SK = OFF_SQ + SW_Q_W
OFF_SV = OFF_SK + SW_KV_W
OFF_GA = OFF_SV + SW_KV_W
OFF_GB = OFF_GA + D_MODEL
SW_PART_W = OFF_GB + D_MODEL
ORIG_B = DN_PART_W
ORIG_A = ORIG_B + DN_HEADS
ORIG_SQ = ORIG_A + DN_HEADS

FF_TN = 256
DOWN_TK = D_FF // 2
OUT_TK = D_MODEL // 2

DN_STACK = 2
DN_GROUP_BATCH = 8
SW_PAIR = 2


def _cparams(sem):
    return pltpu.CompilerParams(dimension_semantics=sem, vmem_limit_bytes=VMEM_LIMIT)


def _row_rsqrt(ssq_ref, d):
    return lax.rsqrt(ssq_ref[:, 0:1] * (1.0 / d) + EPS)


def _prenorm_kernel(x_ref, w_ref, xs_ref, ssq_ref):
    x = x_ref[...]
    xs_ref[...] = (x * w_ref[...]).astype(xs_ref.dtype)
    ssq_ref[...] = jnp.broadcast_to(jnp.sum(x * x, -1, keepdims=True), ssq_ref.shape)


def _prenorm(x, w):
    m, d = x.shape
    tr = TR_ALL
    return pl.pallas_call(
        _prenorm_kernel,
        out_shape=(jax.ShapeDtypeStruct((m, d), BF16), jax.ShapeDtypeStruct((m, LANES), F32)),
        grid=(m // tr,),
        in_specs=[pl.BlockSpec((tr, d), lambda i: (i, 0)), pl.BlockSpec((1, d), lambda i: (0, 0))],
        out_specs=(pl.BlockSpec((tr, d), lambda i: (i, 0)), pl.BlockSpec((tr, LANES), lambda i: (i, 0))),
        compiler_params=_cparams(("parallel",)),
        name="prenorm",
    )(x, w.reshape(1, d))


def _rmsnorm_kernel(x_ref, w_ref, o_ref):
    x = x_ref[...]
    y = x * lax.rsqrt(jnp.mean(x * x, -1, keepdims=True) + EPS)
    o_ref[...] = (y * w_ref[...]).astype(o_ref.dtype)


def _rmsnorm(x, w, out_dtype, row0, nrows, tr):
    d = x.shape[1]
    blk0 = row0 // tr
    return pl.pallas_call(
        _rmsnorm_kernel,
        out_shape=jax.ShapeDtypeStruct((nrows, d), out_dtype),
        grid=(nrows // tr,),
        in_specs=[pl.BlockSpec((tr, d), lambda i: (blk0 + i, 0)), pl.BlockSpec((1, d), lambda i: (0, 0))],
        out_specs=pl.BlockSpec((tr, d), lambda i: (i, 0)),
        compiler_params=_cparams(("parallel",)),
        name="rmsnorm",
    )(x, w.reshape(1, d))


def _mm_kernel(a_ref, ssq_ref, w_ref, o_ref):
    d = jnp.dot(a_ref[...], w_ref[...], preferred_element_type=F32)
    o_ref[...] = (d * _row_rsqrt(ssq_ref, a_ref.shape[1])).astype(o_ref.dtype)


def _normed_matmul(xs, ssq, w, layer, col0, ncols, out_dtype, tn, name):
    m, k = xs.shape
    blk0 = col0 // tn
    return pl.pallas_call(
        _mm_kernel,
        out_shape=jax.ShapeDtypeStruct((m, ncols), out_dtype),
        grid=(m // TM, ncols // tn),
        in_specs=[pl.BlockSpec((TM, k), lambda i, j: (i, 0)),
                  pl.BlockSpec((TM, LANES), lambda i, j: (i, 0)),
                  pl.BlockSpec((None, k, tn), lambda i, j: (layer, 0, blk0 + j))],
        out_specs=pl.BlockSpec((TM, tn), lambda i, j: (i, j)),
        compiler_params=_cparams(("parallel", "parallel")),
        name=name,
    )(xs, ssq, w)


def _swiglu_kernel(a_ref, ssq_ref, wg_ref, wu_ref, o_ref, wgb_ref, wub_ref):
    @pl.when(pl.program_id(1) == 0)
    def _():
        wgb_ref[...] = wg_ref[...].astype(BF16)
        wub_ref[...] = wu_ref[...].astype(BF16)

    a = a_ref[...]
    r = _row_rsqrt(ssq_ref, a_ref.shape[1])
    g = jnp.dot(a, wgb_ref[...], preferred_element_type=F32) * r
    u = jnp.dot(a, wub_ref[...], preferred_element_type=F32) * r
    o_ref[...] = (g * jax.nn.sigmoid(g) * u).astype(o_ref.dtype)


def _swiglu_up(xs, ssq, wg, wu, layer):
    m, k = xs.shape
    n = wg.shape[2]
    tn = FF_TN
    nj = n // tn

    def wspec(switch_row):
        return pl.BlockSpec((None, k, tn),
                            lambda j, i: (layer, 0, jnp.minimum(j + (i >= switch_row).astype(jnp.int32), nj - 1)))

    return pl.pallas_call(
        _swiglu_kernel,
        out_shape=jax.ShapeDtypeStruct((m, n), BF16),
        grid=(n // tn, m // TM),
        in_specs=[pl.BlockSpec((TM, k), lambda j, i: (i, 0)),
                  pl.BlockSpec((TM, LANES), lambda j, i: (i, 0)), wspec(1), wspec(4)],
        out_specs=pl.BlockSpec((TM, tn), lambda j, i: (i, j)),
        scratch_shapes=[pltpu.VMEM((k, tn), BF16), pltpu.VMEM((k, tn), BF16)],
        compiler_params=_cparams(("parallel", "arbitrary")),
        name="ffn_gate_up",
    )(xs, ssq, wg, wu)


def _residual_mm_kernel(a_ref, w_ref, r_ref, ln_ref, o_ref, xs_ref, ssq_ref, *, scale, nk):
    j = pl.program_id(1)
    kk = pl.program_id(2)
    d = jnp.dot(a_ref[...], w_ref[...], preferred_element_type=F32)
    if scale != 1.0:
        d = d * scale

    def finish(val):
        o_ref[...] = val
        xs_ref[...] = (val * ln_ref[...]).astype(xs_ref.dtype)
        part = jnp.broadcast_to(jnp.sum(val * val, -1, keepdims=True), ssq_ref.shape)

        @pl.when(j == 0)
        def _():
            ssq_ref[...] = part

        @pl.when(j != 0)
        def _():
            ssq_ref[...] += part

    if nk == 1:
        finish(r_ref[...] + d)
    else:
        @pl.when(kk == 0)
        def _():
            o_ref[...] = r_ref[...] + d

        @pl.when((kk > 0) & (kk < nk - 1))
        def _():
            o_ref[...] += d

        @pl.when(kk == nk - 1)
        def _():
            finish(o_ref[...] + d)


def _residual_matmul(a, w, layer, res, ln_next, scale, tk, tn, name):
    m, k = a.shape
    n = w.shape[2]
    nk = k // tk
    return pl.pallas_call(
        functools.partial(_residual_mm_kernel, scale=scale, nk=nk),
        out_shape=(jax.ShapeDtypeStruct((m, n), F32), jax.ShapeDtypeStruct((m, n), BF16),
                   jax.ShapeDtypeStruct((m, LANES), F32)),
        grid=(m // TM, n // tn, nk),
        in_specs=[pl.BlockSpec((TM, tk), lambda i, j, kk: (i, kk)),
                  pl.BlockSpec((None, tk, tn), lambda i, j, kk: (layer, kk, j)),
                  pl.BlockSpec((TM, tn), lambda i, j, kk: (i, j)),
                  pl.BlockSpec((1, tn), lambda i, j, kk: (0, j))],
        out_specs=(pl.BlockSpec((TM, tn), lambda i, j, kk: (i, j)),
                   pl.BlockSpec((TM, tn), lambda i, j, kk: (i, j)),
                   pl.BlockSpec((TM, LANES), lambda i, j, kk: (i, 0))),
        compiler_params=_cparams(("parallel", "arbitrary", "arbitrary")),
        name=name,
    )(a, w, res, ln_next.reshape(1, n))


def _merge_kernel(oa_ref, ob_ref, wa_ref, wb_ref, ga_ref, gb_ref, o_ref):
    ya = jnp.dot(oa_ref[...], wa_ref[...], preferred_element_type=F32)
    yb = jnp.dot(ob_ref[...], wb_ref[...], preferred_element_type=F32)
    o = jax.nn.sigmoid(ga_ref[...]) * ya + jax.nn.sigmoid(gb_ref[...]) * yb
    o_ref[...] = o.astype(o_ref.dtype)


def _branch_merge(o_a, o_b, w_a, w_b, layer, proj):
    m, ka = o_a.shape
    kb = o_b.shape[1]
    n = w_a.shape[2]
    tn = TN_HALF
    ga_blk = OFF_GA // tn
    gb_blk = OFF_GB // tn
    return pl.pallas_call(
        _merge_kernel,
        out_shape=jax.ShapeDtypeStruct((m, n), BF16),
        grid=(m // TM, n // tn),
        in_specs=[pl.BlockSpec((TM, ka), lambda i, j: (i, 0)),
                  pl.BlockSpec((TM, kb), lambda i, j: (i, 0)),
                  pl.BlockSpec((None, ka, tn), lambda i, j: (layer, 0, j)),
                  pl.BlockSpec((None, kb, tn), lambda i, j: (layer, 0, j)),
                  pl.BlockSpec((TM, tn), lambda i, j: (i, ga_blk + j)),
                  pl.BlockSpec((TM, tn), lambda i, j: (i, gb_blk + j))],
        out_specs=pl.BlockSpec((TM, tn), lambda i, j: (i, j)),
        compiler_params=_cparams(("parallel", "parallel")),
        name="branch_merge",
    )(o_a, o_b, w_a, w_b, proj, proj)


def _dot_nt(a, b, precision=None):
    return lax.dot_general(a, b, (((1,), (1,)), ((), ())), precision=precision,
                           preferred_element_type=F32)


def _dot_tn(a, b):
    return lax.dot_general(a, b, (((0,), (0,)), ((), ())), preferred_element_type=F32)


def _dot_hi(a, b):
    return jnp.dot(a, b, precision=HIGHEST, preferred_element_type=F32)


def _split(x):
    hi = x.astype(BF16)
    return hi, (x - hi.astype(F32)).astype(BF16)


def _bdot(a, b):
    return jnp.einsum("gij,gjk->gik", a.astype(BF16), b.astype(BF16), preferred_element_type=F32)


def _bdot_nt(a, b):
    return jnp.einsum("gik,gjk->gij", a.astype(BF16), b.astype(BF16), preferred_element_type=F32)


def _bdot_3x(a, b):
    return _bdot(a[0], b[0]) + _bdot(a[0], b[1]) + _bdot(a[1], b[0])


def _unit_lower_inverse_minus_eye(a, n, c):
    ri = lax.broadcasted_iota(jnp.int32, (n, n), 0)
    ci = lax.broadcasted_iota(jnp.int32, (n, n), 1)
    base = min(16, c)
    p = jnp.where((ri // base) == (ci // base), a, 0.0)
    nt = -p
    for _ in range(int(math.log2(base)) - 1):
        p = _bdot(p, p)
        nt = nt + p + _bdot(nt, p)
    blk = base
    while blk < c:
        pair = (ri // (2 * blk)) == (ci // (2 * blk))
        lower = pair & (((ri // blk) % 2) == 1) & (((ci // blk) % 2) == 0)
        l = jnp.where(lower, a, 0.0)
        lt = l + _bdot(l, nt)
        nt = nt - lt - _bdot(nt, lt)
        blk *= 2
    return nt


def _dn_kernel(qc_ref, kc_ref, vc_ref, qp_ref, kp_ref, vp_ref, qi_ref, ki_ref, vi_ref,
               z_ref, b_ref, a_ref, wq_ref, wk_ref, wv_ref, alog_ref, dtb_ref, nw_ref, s0_ref,
               o_ref, sout_ref, s_ref, ext_ref, *, c, t_valid):
    ch = pl.program_id(1)
    nch = pl.num_programs(1)
    n = DN_STACK * c

    @pl.when(ch == 0)
    def _():
        s_ref[...] = s0_ref[0]

    def conv(cur_ref, prev_ref, init_ref, w_ref):
        prev = jnp.where(ch == 0, init_ref[0], prev_ref[...])
        ext_ref[0:SUBLANES, :] = prev
        ext_ref[SUBLANES:SUBLANES + c, :] = cur_ref[...]
        first = SUBLANES - (DN_CONV - 1)
        y = ext_ref[first:first + c, :] * w_ref[0:1, :]
        for j in range(1, DN_CONV):
            y = y + ext_ref[first + j:first + j + c, :] * w_ref[j:j + 1, :]
        return y * jax.nn.sigmoid(y)

    yq = conv(qc_ref, qp_ref, qi_ref, wq_ref)
    yk = conv(kc_ref, kp_ref, ki_ref, wk_ref)
    yv = conv(vc_ref, vp_ref, vi_ref, wv_ref)
    zz = z_ref[...]

    beta_all = jax.nn.sigmoid(b_ref[...])
    araw = a_ref[...] + dtb_ref[...]
    softplus = jnp.maximum(araw, 0.0) + jnp.log1p(jnp.exp(-jnp.abs(araw)))
    g_all = -jnp.exp(alog_ref[...]) * softplus
    if t_valid < c:
        row_ok = lax.broadcasted_iota(jnp.int32, (c, LANES), 0) < t_valid
        beta_all = jnp.where(row_ok, beta_all, 0.0)
        g_all = jnp.where(row_ok, g_all, 0.0)
    tri = lax.broadcasted_iota(jnp.int32, (c, c), 0) >= lax.broadcasted_iota(jnp.int32, (c, c), 1)
    g_cum = _dot_hi(tri.astype(F32), g_all)
    lri = lax.broadcasted_iota(jnp.int32, (LANES, LANES), 0)
    lci = lax.broadcasted_iota(jnp.int32, (LANES, LANES), 1)
    g_cum_t = _dot_nt((lri == lci).astype(F32), g_cum, HIGHEST)

    ri = lax.broadcasted_iota(jnp.int32, (n, n), 0)
    ci = lax.broadcasted_iota(jnp.int32, (n, n), 1)
    same_head = (ri // c) == (ci // c)
    causal = same_head & ((ri % c) >= (ci % c))
    strict = same_head & ((ri % c) > (ci % c))

    for g0 in range(0, DN_HEADS // DN_STACK, DN_GROUP_BATCH):
        groups = [[(g0 + gi) * DN_STACK + j for j in range(DN_STACK)] for gi in range(DN_GROUP_BATCH)]

        def stack(y):
            return jnp.stack([jnp.concatenate([y[:, h * DN_DK:(h + 1) * DN_DK] for h in hs], axis=0) for hs in groups])

        def column(m):
            return jnp.stack([jnp.concatenate([m[:, h:h + 1] for h in hs], axis=0) for hs in groups])

        q = stack(yq)
        k = stack(yk)
        v = stack(yv)
        bcol = column(beta_all)
        gcol = column(g_cum)
        grow = jnp.stack([jnp.concatenate([g_cum_t[h:h + 1, :] for h in hs], axis=1) for hs in groups])
        glast = jnp.stack([jnp.concatenate([jnp.broadcast_to(g_cum[c - 1:c, h:h + 1], (c, 1)) for h in hs], axis=0)
                           for hs in groups])
        q = q * lax.rsqrt(jnp.sum(q * q, -1, keepdims=True) + 1e-6) * (DN_DK ** -0.5)
        k = k * lax.rsqrt(jnp.sum(k * k, -1, keepdims=True) + 1e-6)
        decay = jnp.exp(jnp.where(causal, gcol - grow, -jnp.inf))
        kb = k * bcol
        a_mat = jnp.where(strict, _bdot_nt(kb, k) * decay, 0.0)
        t_low = _unit_lower_inverse_minus_eye(a_mat, n, c)
        eg = jnp.exp(gcol)
        rhs = jnp.concatenate([v * bcol, kb * eg], axis=2)
        sol = rhs + _bdot_3x(_split(t_low), _split(rhs))
        u_base = sol[:, :, :DN_DV]
        w_cum = sol[:, :, DN_DV:].astype(BF16)
        a_qk = _bdot_nt(q, k) * decay
        q_dec = (q * eg).astype(BF16)
        k_dec = (k * jnp.exp(glast - gcol)).astype(BF16)

        s_old = [[s_ref[h] for h in hs] for hs in groups]
        ws, qs = [], []
        for gi, hs in enumerate(groups):
            wg, qg = [], []
            for j in range(DN_STACK):
                rows = slice(j * c, (j + 1) * c)
                lhs = jnp.concatenate([w_cum[gi, rows], q_dec[gi, rows]], axis=0)
                r = jnp.dot(lhs, s_old[gi][j].astype(BF16), preferred_element_type=F32)
                wg.append(r[:c])
                qg.append(r[c:])
            ws.append(jnp.concatenate(wg, axis=0))
            qs.append(jnp.concatenate(qg, axis=0))
        u = u_base - jnp.stack(ws)
        u_b = u.astype(BF16)
        o = jnp.stack(qs) + _bdot(a_qk, u_b)
        for gi, hs in enumerate(groups):
            for j, h in enumerate(hs):
                rows = slice(j * c, (j + 1) * c)
                g_tot = jnp.exp(g_cum[c - 1:c, h:h + 1])
                s_ref[h] = s_old[gi][j] * g_tot + _dot_tn(k_dec[gi, rows], u_b[gi, rows])
        o = o * lax.rsqrt(jnp.mean(o * o, -1, keepdims=True) + EPS) * nw_ref[...]
        z = stack(zz)
        o = o * (z * jax.nn.sigmoid(z))
        for gi, hs in enumerate(groups):
            for j, h in enumerate(hs):
                o_ref[:, h * DN_DV:(h + 1) * DN_DV] = o[gi, j * c:(j + 1) * c].astype(o_ref.dtype)

    @pl.when(ch == nch - 1)
    def _():
        sout_ref[0] = s_ref[...]


def _deltanet(proj, gates, conv_init, s0, conv_w, a_log, dt_bias, norm_w, *, nb, t, c, t_valid, row0):
    w = DN_QK_W
    nch = t // c
    cur_row = lambda b, k: row0 // c + b * nch + k
    prev_row = lambda b, k: jnp.maximum(row0 // SUBLANES + (b * t + k * c) // SUBLANES - 1, 0)
    kq, kk, kv, kz = (OFF_QKV // w, (OFF_QKV + DN_QK_W) // w, (OFF_QKV + 2 * DN_QK_W) // w, OFF_Z // w)

    def cur(col):
        return pl.BlockSpec((c, w), lambda b, k: (cur_row(b, k), col))

    def prev(col):
        return pl.BlockSpec((SUBLANES, w), lambda b, k: (prev_row(b, k), col))

    def init(col):
        return pl.BlockSpec((1, SUBLANES, w), lambda b, k: (b, 0, col))

    def taps(col):
        return pl.BlockSpec((DN_CONV, w), lambda b, k: (0, col))

    row = lambda: pl.BlockSpec((1, LANES), lambda b, k: (0, 0))
    pad = lambda x: jnp.zeros((1, LANES), F32).at[0, :DN_HEADS].set(x.astype(F32))
    state = pl.BlockSpec((1, DN_HEADS, DN_DK, DN_DV), lambda b, k: (b, 0, 0, 0))
    o, s_new = pl.pallas_call(
        functools.partial(_dn_kernel, c=c, t_valid=t_valid),
        out_shape=(jax.ShapeDtypeStruct((nb * t, DN_V_W), BF16),
                   jax.ShapeDtypeStruct((nb, DN_HEADS, DN_DK, DN_DV), F32)),
        grid=(nb, nch),
        in_specs=[cur(kq), cur(kk), cur(kv), prev(kq), prev(kk), prev(kv),
                  init(kq), init(kk), init(kv), cur(kz),
                  pl.BlockSpec((c, LANES), lambda b, k: (cur_row(b, k), 0)),
                  pl.BlockSpec((c, LANES), lambda b, k: (cur_row(b, k), 1)),
                  taps(kq), taps(kk), taps(kv), row(), row(), row(), state],
        out_specs=(pl.BlockSpec((c, w), lambda b, k: (b * nch + k, 0)), state),
        scratch_shapes=[pltpu.VMEM((DN_HEADS, DN_DK, DN_DV), F32),
                        pltpu.VMEM((c + SUBLANES, w), F32)],
        compiler_params=_cparams(("parallel", "arbitrary")),
        name="deltanet",
    )(proj, proj, proj, proj, proj, proj, conv_init, conv_init, conv_init, proj, gates, gates,
      conv_w, conv_w, conv_w, pad(a_log), pad(dt_bias), norm_w.reshape(1, DN_DV).astype(F32), s0)
    return o, s_new


def _swa_kernel(slopes_ref, sinks_ref, q_ref, kp_ref, kc_ref, vp_ref, vc_ref, o_ref, *, tq, first_prev_valid):
    blk = pl.program_id(1)
    pair = pl.program_id(2)
    tk = WINDOW + tq
    n = SW_GROUP * tq
    k = jnp.concatenate([kp_ref[...], kc_ref[...]], axis=0)
    v = jnp.concatenate([vp_ref[...], vc_ref[...]], axis=0)
    k_swap = pltpu.roll(k, SW_HD, 1)
    v_swap = pltpu.roll(v, SW_HD, 1)
    low_kv = lax.broadcasted_iota(jnp.int32, (tk, LANES), 1) < SW_HD
    q = q_ref[...]
    lane_low = lax.broadcasted_iota(jnp.int32, (n, LANES), 1) < SW_HD
    grp = lax.broadcasted_iota(jnp.int32, (n, 1), 0) // tq
    own_half = lane_low == ((grp % 2) == 0)
    low_out = lax.broadcasted_iota(jnp.int32, (tq, LANES), 1) < SW_HD
    ri = lax.broadcasted_iota(jnp.int32, (n, tk), 0)
    ci = lax.broadcasted_iota(jnp.int32, (n, tk), 1)
    dist = (ri % tq) + WINDOW - ci
    valid = (dist >= 0) & (dist <= WINDOW)
    if not first_prev_valid:
        valid = valid & ((ci >= WINDOW) | (blk > 0))
    neg_dist = jnp.where(valid, -dist.astype(F32), -jnp.inf)
    for kh in range(SW_PAIR):
        head0 = (pair * SW_PAIR + kh) * SW_GROUP
        slope = jnp.zeros((n, 1), F32)
        sink = jnp.zeros((n, 1), F32)
        for g in range(SW_GROUP):
            slope = jnp.where(grp == g, slopes_ref[head0 + g], slope)
            sink = jnp.where(grp == g, sinks_ref[head0 + g], sink)
        kk = (jnp.where(low_kv, k, k_swap) if kh == 0 else jnp.where(low_kv, k_swap, k)).astype(BF16)
        vv = (jnp.where(low_kv, v, v_swap) if kh == 0 else jnp.where(low_kv, v_swap, v)).astype(BF16)
        blocks = [kh * (SW_GROUP // 2) + g // 2 for g in range(SW_GROUP)]
        qs = jnp.concatenate([q[:, b * LANES:(b + 1) * LANES] for b in blocks], axis=0)
        qm = jnp.where(own_half, qs * (SW_HD ** -0.5), 0.0).astype(BF16)
        s = _dot_nt(qm, kk) + slope * neg_dist
        m = jnp.maximum(jnp.max(s, -1, keepdims=True), sink)
        p = jnp.exp(s - m)
        p = p / (jnp.sum(p, -1, keepdims=True) + jnp.exp(sink - m))
        o = jnp.dot(p.astype(BF16), vv, preferred_element_type=F32)
        for j in range(SW_GROUP // 2):
            even = o[(2 * j) * tq:(2 * j + 1) * tq]
            odd = o[(2 * j + 1) * tq:(2 * j + 2) * tq]
            b = kh * (SW_GROUP // 2) + j
            o_ref[:, b * LANES:(b + 1) * LANES] = jnp.where(low_out, even, odd).astype(o_ref.dtype)


def _swa(proj, k_prev_src, v_prev_src, slopes, sinks, *, nb, nblk, tq, row0, prev_from_proj):
    qw = SW_PAIR * SW_GROUP * SW_HD
    kw = SW_PAIR * SW_HD
    qb, kb, vb = OFF_SQ // qw, OFF_SK // kw, OFF_SV // kw
    cur_row = lambda b, n, p: row0 // tq + b * nblk + n
    if prev_from_proj:
        prev_k = pl.BlockSpec((WINDOW, kw), lambda b, n, p: (b * nblk + jnp.maximum(n - 1, 0), kb + p))
        prev_v = pl.BlockSpec((WINDOW, kw), lambda b, n, p: (b * nblk + jnp.maximum(n - 1, 0), vb + p))
    else:
        prev_k = pl.BlockSpec((WINDOW, kw), lambda b, n, p: (b, p))
        prev_v = pl.BlockSpec((WINDOW, kw), lambda b, n, p: (b, p))
    smem = pl.BlockSpec(memory_space=pltpu.SMEM)
    return pl.pallas_call(
        functools.partial(_swa_kernel, tq=tq, first_prev_valid=not prev_from_proj),
        out_shape=jax.ShapeDtypeStruct((nb * nblk * tq, SW_Q_W), BF16),
        grid=(nb, nblk, SW_KV_HEADS // SW_PAIR),
        in_specs=[smem, smem,
                  pl.BlockSpec((tq, qw), lambda b, n, p: (cur_row(b, n, p), qb + p)),
                  prev_k,
                  pl.BlockSpec((tq, kw), lambda b, n, p: (cur_row(b, n, p), kb + p)),
                  prev_v,
                  pl.BlockSpec((tq, kw), lambda b, n, p: (cur_row(b, n, p), vb + p))],
        out_specs=pl.BlockSpec((tq, qw), lambda b, n, p: (b * nblk + n, p)),
        compiler_params=_cparams(("parallel", "parallel", "parallel")),
        name="swa",
    )(slopes, sinks, proj, k_prev_src, proj, v_prev_src, proj)


def _prep_in_proj(w_in):
    w_b = w_in.astype(BF16)
    sw_part = w_b[:, :, ORIG_SQ:]
    side = jnp.zeros(w_in.shape[:2] + (2 * LANES,), BF16)
    side = side.at[:, :, :DN_HEADS].set(w_b[:, :, ORIG_B:ORIG_A])
    side = side.at[:, :, LANES:LANES + DN_HEADS].set(w_b[:, :, ORIG_A:ORIG_SQ])
    return w_b, sw_part, side


def _ffn_half_step(x, xs, ssq, wg, wu, wd, layer, ln_next, name):
    hidden = _swiglu_up(xs, ssq, wg, wu, layer)
    return _residual_matmul(hidden, wd, layer, x, ln_next, 0.5, DOWN_TK, TN_HALF, name)


def _rows(a, r0, r1, c0, c1):
    return lax.slice(a, (r0, c0), (r1, c1))


def _pad_sample(p):
    w = p.shape[1]
    s = _rows(p, M_PROMPT, M_ROWS, 0, w).reshape(DEC_BATCH, DEC_SEQ, w)
    return jnp.pad(s, ((0, 0), (0, DEC_PAD - DEC_SEQ), (0, 0))).reshape(DEC_BATCH * DEC_PAD, w)


def _unpad_sample(o):
    w = o.shape[1]
    return o.reshape(DEC_BATCH, DEC_PAD, w)[:, :DEC_SEQ].reshape(M_SAMPLE, w)


def kernel(x_prompt, x_sample, state_dn_conv, state_dn_recurrent, cache_swa_k, cache_swa_v, ln_ffn1, w_ffn1_gate, w_ffn1_up, w_ffn1_down, ln_mix, w_in, dn_conv_w, dn_a_log, dn_dt_bias, dn_norm_w, swa_sinks, w_branch_a, w_branch_b, w_out, ln_ffn2, w_ffn2_gate, w_ffn2_up, w_ffn2_down, ln_final):
    slopes = 2.0 ** (-8.0 * jnp.arange(1, SW_HEADS + 1, dtype=F32) / SW_HEADS)
    x = jnp.concatenate([x_prompt.reshape(M_PROMPT, D_MODEL), x_sample.reshape(M_SAMPLE, D_MODEL)], axis=0)
    zero_conv = jnp.zeros((BATCH, SUBLANES, DN_CONV_DIM), F32)
    zero_state = jnp.zeros((BATCH, DN_HEADS, DN_DK, DN_DV), F32)

    wg1, wu1, wd1 = w_ffn1_gate, w_ffn1_up, w_ffn1_down.astype(BF16)
    wg2, wu2, wd2 = w_ffn2_gate, w_ffn2_up, w_ffn2_down.astype(BF16)
    w_inb, w_sw, w_side = _prep_in_proj(w_in)
    w_a = w_branch_a.astype(BF16)
    w_b = w_branch_b.astype(BF16)
    w_o = w_out.astype(BF16)
    buf = cache_swa_k.shape[2]
    keep = min(WINDOW, SEQ)

    outs = [[] for _ in range(8)]
    xs, ssq = _prenorm(x, ln_ffn1[0])
    for l in range(DEPTH):
        x, xs, ssq = _ffn_half_step(x, xs, ssq, wg1, wu1, wd1, l, ln_mix[l], "ffn1_down")

        proj_dn = _normed_matmul(xs, ssq, w_inb, l, 0, DN_PART_W, F32, TN, "in_proj_dn")
        proj_sw = _normed_matmul(xs, ssq, w_sw, l, 0, SW_PART_W, F32, TN, "in_proj_sw")
        gates = _normed_matmul(xs, ssq, w_side, l, 0, 2 * LANES, F32, 2 * LANES, "in_proj_gates")

        conv_s = jnp.pad(state_dn_conv[l], ((0, 0), (SUBLANES - (DN_CONV - 1), 0), (0, 0)))
        dn_args = (dn_conv_w[l], dn_a_log[l], dn_dt_bias[l], dn_norm_w[l])
        oa_p, rec_p = _deltanet(proj_dn, gates, zero_conv, zero_state, *dn_args,
                                nb=BATCH, t=SEQ, c=DN_CHUNK, t_valid=DN_CHUNK, row0=0)
        oa_s, rec_s = _deltanet(_pad_sample(proj_dn), _pad_sample(gates), conv_s,
                                state_dn_recurrent[l].astype(F32), *dn_args,
                                nb=DEC_BATCH, t=DEC_PAD, c=DEC_PAD, t_valid=DEC_SEQ, row0=0)

        ck = cache_swa_k[l].reshape(DEC_BATCH * buf, SW_KV_W)
        cv = cache_swa_v[l].reshape(DEC_BATCH * buf, SW_KV_W)
        sinks = swa_sinks[l].astype(F32)
        ob_p = _swa(proj_sw, proj_sw, proj_sw, slopes, sinks, nb=BATCH, nblk=SEQ // WINDOW,
                    tq=WINDOW, row0=0, prev_from_proj=True)
        ob_s = _swa(_pad_sample(proj_sw), ck, cv, slopes, sinks, nb=DEC_BATCH, nblk=1,
                    tq=DEC_PAD, row0=0, prev_from_proj=False)

        o_a = jnp.concatenate([oa_p, _unpad_sample(oa_s)], axis=0)
        o_b = jnp.concatenate([ob_p, _unpad_sample(ob_s)], axis=0)
        merged = _branch_merge(o_a, o_b, w_a, w_b, l, proj_sw)
        x, xs, ssq = _residual_matmul(merged, w_o, l, x, ln_ffn2[l], 1.0, OUT_TK, TN, "out_proj")

        ln_next = ln_ffn1[l + 1] if l + 1 < DEPTH else ln_final
        x, xs, ssq = _ffn_half_step(x, xs, ssq, wg2, wu2, wd2, l, ln_next, "ffn2_down")

        last = [(b + 1) * SEQ for b in range(BATCH)]
        sample = lambda p, c0, c1: _rows(p, M_PROMPT, M_ROWS, c0, c1).reshape(DEC_BATCH, DEC_SEQ, c1 - c0)
        outs[0].append(jnp.stack([_rows(proj_dn, e - (DN_CONV - 1), e, OFF_QKV, OFF_QKV + DN_CONV_DIM) for e in last]))
        outs[1].append(rec_p)
        outs[2].append(jnp.stack([_rows(proj_sw, e - keep, e, OFF_SK, OFF_SK + SW_KV_W) for e in last])
                       .reshape(BATCH, keep, SW_KV_HEADS, SW_HD))
        outs[3].append(jnp.stack([_rows(proj_sw, e - keep, e, OFF_SV, OFF_SV + SW_KV_W) for e in last])
                       .reshape(BATCH, keep, SW_KV_HEADS, SW_HD))
        outs[4].append(sample(proj_dn, OFF_QKV, OFF_QKV + DN_CONV_DIM)[:, DEC_SEQ - (DN_CONV - 1):])
        outs[5].append(rec_s)
        new_k = sample(proj_sw, OFF_SK, OFF_SK + SW_KV_W).reshape(DEC_BATCH, DEC_SEQ, SW_KV_HEADS, SW_HD)
        new_v = sample(proj_sw, OFF_SV, OFF_SV + SW_KV_W).reshape(DEC_BATCH, DEC_SEQ, SW_KV_HEADS, SW_HD)
        outs[6].append(jnp.concatenate([cache_swa_k[l], new_k], axis=1)[:, -buf:])
        outs[7].append(jnp.concatenate([cache_swa_v[l], new_v], axis=1)[:, -buf:])

    y_prompt = _rmsnorm(x, ln_final, F32, 0, M_PROMPT, TR_PROMPT).reshape(BATCH, SEQ, D_MODEL)
    y_sample = _rmsnorm(x, ln_final, F32, M_PROMPT, M_SAMPLE, TR_SAMPLE).reshape(DEC_BATCH, DEC_SEQ, D_MODEL)
    rec_dtype = state_dn_recurrent.dtype
    return (y_prompt, y_sample,
            jnp.stack(outs[0]), jnp.stack(outs[1]).astype(rec_dtype), jnp.stack(outs[2]), jnp.stack(outs[3]),
            jnp.stack(outs[4]), jnp.stack(outs[5]).astype(rec_dtype), jnp.stack(outs[6]), jnp.stack(outs[7]))
```
